```python
import jax
import jax.numpy as jnp
from jax import lax
import numpy as np

D_MODEL = 1024
BATCH = 8
SEQ = 2048
DEPTH = 4

N_META = 16
CHUNK = 64
PAD_FRONT = (-N_META) % CHUNK
BRANCH_WIDTH = 512
N_BRANCH = 4
D_FF = 4 * D_MODEL
NORM_EPS = 1e-6

RET_HEADS = 4
RET_DK = 64
RET_DV = 128
RET_ROPE_BASE = 10000.0

S5_GROUP = 16
S5_GROUPS = BRANCH_WIDTH // S5_GROUP
S5_STATE = 64

SSD_HEADDIM = 64
SSD_HEADS = BRANCH_WIDTH // SSD_HEADDIM
SSD_GROUPS = 2
SSD_STATE = 128
SSD_CONV = 4
SSD_CONV_DIM = BRANCH_WIDTH + 2 * SSD_GROUPS * SSD_STATE

HG_HEADS = 4
HG_DK = BRANCH_WIDTH // HG_HEADS
HG_DV = BRANCH_WIDTH // HG_HEADS

IN_SIZES = (RET_HEADS * RET_DK, RET_HEADS * RET_DK, RET_HEADS * RET_DV, RET_HEADS * RET_DV,
            BRANCH_WIDTH,
            BRANCH_WIDTH, SSD_CONV_DIM, SSD_HEADS,
            HG_HEADS * HG_DK, HG_HEADS * HG_DK, HG_HEADS * HG_DV, HG_HEADS * HG_DV,
            N_BRANCH * D_MODEL)
IN_DIM = sum(IN_SIZES)

kernel_name = 'hybrid_ret_s5_ssd_hgrn2_trunk'


def rms_norm(x, w):
    xf = x.astype(jnp.float32)
    y = xf * lax.rsqrt(jnp.mean(xf * xf, axis=-1, keepdims=True) + NORM_EPS)
    return (y * w.astype(jnp.float32)).astype(x.dtype)


def to_chunks(t):
    pad = [(0, 0)] * t.ndim
    pad[1] = (PAD_FRONT, 0)
    t = jnp.pad(t, pad)
    b, tt = t.shape[0], t.shape[1]
    t = t.reshape((b, tt // CHUNK, CHUNK) + t.shape[2:])
    return jnp.moveaxis(t, 1, 0)


def from_chunks(y):
    y = jnp.moveaxis(y, 0, 1)
    b, n = y.shape[0], y.shape[1]
    y = y.reshape((b, n * CHUNK) + y.shape[3:])
    return y[:, PAD_FRONT:]


def chunked_scalar_decay_attention(q, k, v, log_a):
    b, _, g, dk = q.shape
    hg, dv = v.shape[3], v.shape[4]
    causal = jnp.tril(jnp.ones((CHUNK, CHUNK), dtype=bool))[None, :, :, None, None]

    def step(state, inp):
        qc, kc, vc, lac = inp
        cum = jnp.cumsum(lac.astype(jnp.float32), axis=1)
        seg = cum[:, :, None] - cum[:, None, :]
        decay = jnp.exp(jnp.where(causal, seg, -jnp.inf)).astype(qc.dtype)
        scores = jnp.einsum('btgd,bsgd->btsg', qc, kc)
        y_in = jnp.einsum('btsgh,bsghe->btghe', scores[..., None] * decay, vc)
        y_x = jnp.einsum('btgd,bghde->btghe', qc, state) * jnp.exp(cum)[..., None]
        last = cum[:, -1]
        w = jnp.exp(last[:, None] - cum).astype(kc.dtype)
        state = state * jnp.exp(last)[..., None, None] + jnp.einsum('bsgd,bsgh,bsghe->bghde', kc, w, vc)
        return state, y_in + y_x

    state0 = jnp.zeros((b, g, hg, dk, dv), jnp.float32)
    _, y = lax.scan(step, state0, (to_chunks(q), to_chunks(k), to_chunks(v), to_chunks(log_a)))
    return from_chunks(y).astype(v.dtype)


def chunked_vector_decay_attention(q, k, v, log_f):
    b, _, h, dk = q.shape
    dv = v.shape[-1]
    causal = jnp.tril(jnp.ones((CHUNK, CHUNK), dtype=bool))[None, :, :, None, None]

    def step(state, inp):
        qc, kc, vc, lfc = inp
        cum = jnp.cumsum(lfc.astype(jnp.float32), axis=1)
        decay = jnp.exp(jnp.where(causal, cum[:, :, None] - cum[:, None, :], -jnp.inf)).astype(qc.dtype)
        scores = jnp.einsum('bthd,bshd,btshd->btsh', qc, kc, decay)
        y_in = jnp.einsum('btsh,bshe->bthe', scores, vc)
        y_x = jnp.einsum('bthd,bhde->bthe', qc * jnp.exp(cum).astype(qc.dtype), state)
        last = cum[:, -1]
        kw = kc * jnp.exp(last[:, None] - cum).astype(kc.dtype)
        state = state * jnp.exp(last)[..., None] + jnp.einsum('bshd,bshe->bhde', kw, vc)
        return state, y_in + y_x

    state0 = jnp.zeros((b, h, dk, dv), jnp.float32)
    _, y = lax.scan(step, state0, (to_chunks(q), to_chunks(k), to_chunks(v), to_chunks(log_f)))
    return from_chunks(y).astype(v.dtype)


def rotary(x):
    t, dk = x.shape[1], x.shape[-1]
    half = dk // 2
    inv_freq = RET_ROPE_BASE ** (-jnp.arange(half, dtype=jnp.float32) / half)
    ang = jnp.arange(t, dtype=jnp.float32)[:, None] * inv_freq[None, :]
    cos = jnp.cos(ang)[None, :, None, :].astype(x.dtype)
    sin = jnp.sin(ang)[None, :, None, :].astype(x.dtype)
    x1, x2 = x[..., :half], x[..., half:]
    return jnp.concatenate([x1 * cos - x2 * sin, x1 * sin + x2 * cos], axis=-1)


def retention_mixer(q, k, v, g, gn_w):
    bsz, t, _ = q.shape
    q = rotary(q.reshape(bsz, t, RET_HEADS, RET_DK))
    k = rotary(k.reshape(bsz, t, RET_HEADS, RET_DK)) * RET_DK ** -0.5
    v = v.reshape(bsz, t, RET_HEADS, 1, RET_DV)
    gamma = 1.0 - jnp.exp2(-5.0 - jnp.arange(RET_HEADS, dtype=jnp.float32))
    log_a = jnp.broadcast_to(jnp.log(gamma)[:, None], (bsz, t, RET_HEADS, 1))
    y = chunked_scalar_decay_attention(q, k, v, log_a).reshape(bsz, t, RET_HEADS, RET_DV)
    yf = y.astype(jnp.float32)
    mu = jnp.mean(yf, axis=-1, keepdims=True)
    var = jnp.mean(jnp.square(yf - mu), axis=-1, keepdims=True)
    yn = ((yf - mu) * lax.rsqrt(var + NORM_EPS)).reshape(bsz, t, BRANCH_WIDTH) * gn_w.astype(jnp.float32)
    return jax.nn.silu(g) * yn.astype(g.dtype)


def s5_mixer(u, lam_re, lam_im, b_re, b_im, c_re, c_im, d_skip, log_step, glu_w, glu_b):
    bsz, t, _ = u.shape
    uf = u.astype(jnp.float32).reshape(bsz, t, S5_GROUPS, S5_GROUP)
    step = jnp.exp(log_step.astype(jnp.float32))[:, None]
    lr, li = lam_re.astype(jnp.float32), lam_im.astype(jnp.float32)
    mag = jnp.exp(lr * step)
    ab_re, ab_im = mag * jnp.cos(li * step), mag * jnp.sin(li * step)
    inv = 1.0 / (lr * lr + li * li)
    co_re = ((ab_re - 1.0) * lr + ab_im * li) * inv
    co_im = (ab_im * lr - (ab_re - 1.0) * li) * inv
    br, bi = b_re.astype(jnp.float32), b_im.astype(jnp.float32)
    bb_re = co_re[..., None] * br - co_im[..., None] * bi
    bb_im = co_re[..., None] * bi + co_im[..., None] * br
    bu_re = jnp.einsum('btgj,gpj->btgp', uf, bb_re)
    bu_im = jnp.einsum('btgj,gpj->btgp', uf, bb_im)
    a_re = jnp.broadcast_to(ab_re, bu_re.shape)
    a_im = jnp.broadcast_to(ab_im, bu_im.shape)

    def combine(e1, e2):
        a1r, a1i, b1r, b1i = e1
        a2r, a2i, b2r, b2i = e2
        return (a2r * a1r - a2i * a1i, a2r * a1i + a2i * a1r,
                a2r * b1r - a2i * b1i + b2r, a2r * b1i + a2i * b1r + b2i)

    _, _, xr, xi = lax.associative_scan(combine, (a_re, a_im, bu_re, bu_im), axis=1)
    y = (jnp.einsum('btgp,gjp->btgj', xr, c_re.astype(jnp.float32))
         - jnp.einsum('btgp,gjp->btgj', xi, c_im.astype(jnp.float32)))
    y = y.reshape(bsz, t, BRANCH_WIDTH) + d_skip.astype(jnp.float32) * u.astype(jnp.float32)
    y = jax.nn.gelu(y).astype(u.dtype)
    a, gt = jnp.split(y @ glu_w + glu_b, 2, axis=-1)
    return a * jax.nn.sigmoid(gt)


def ssd_mixer(z, xbc, dt_raw, conv_w, conv_b, dt_bias, a_log, d_skip, norm_w):
    bsz, t, _ = z.shape
    hg = SSD_HEADS // SSD_GROUPS
    xbc = lax.conv_general_dilated(xbc, conv_w[:, None, :], window_strides=(1,), padding=[(SSD_CONV - 1, 0)],
                                   dimension_numbers=('NWC', 'WIO', 'NWC'),
                                   feature_group_count=SSD_CONV_DIM) + conv_b
    xbc = jax.nn.silu(xbc)
    xs, bm, cm = jnp.split(xbc, [BRANCH_WIDTH, BRANCH_WIDTH + SSD_GROUPS * SSD_STATE], axis=-1)
    xs = xs.reshape(bsz, t, SSD_GROUPS, hg, SSD_HEADDIM)
    bm = bm.reshape(bsz, t, SSD_GROUPS, SSD_STATE)
    cm = cm.reshape(bsz, t, SSD_GROUPS, SSD_STATE)
    dt = jax.nn.softplus((dt_raw + dt_bias).astype(jnp.float32)).reshape(bsz, t, SSD_GROUPS, hg)
    a = -jnp.exp(a_log.astype(jnp.float32)).reshape(SSD_GROUPS, hg)
    y = chunked_scalar_decay_attention(cm, bm, xs * dt[..., None].astype(xs.dtype), dt * a)
    y = y + d_skip.reshape(SSD_GROUPS, hg)[..., None] * xs
    y = y.reshape(bsz, t, BRANCH_WIDTH) * jax.nn.silu(z)
    y = rms_norm(y.reshape(bsz, t, SSD_GROUPS, -1), norm_w.reshape(SSD_GROUPS, -1))
    return y.reshape(bsz, t, BRANCH_WIDTH)


def hgrn2_mixer(q, f_raw, i, g, lb, norm_w):
    bsz, t, _ = q.shape
    q = jax.nn.silu(q).reshape(bsz, t, HG_HEADS, HG_DK)
    f = lb + (1.0 - lb) * jax.nn.sigmoid(f_raw.astype(jnp.float32))
    log_f = jnp.log(f).reshape(bsz, t, HG_HEADS, HG_DK)
    k = (1.0 - f).astype(q.dtype).reshape(bsz, t, HG_HEADS, HG_DK)
    v = i.reshape(bsz, t, HG_HEADS, HG_DV)
    o = chunked_vector_decay_attention(q, k, v, log_f)
    o = rms_norm(o, norm_w.reshape(HG_HEADS, HG_DV)).reshape(bsz, t, BRANCH_WIDTH)
    return o * jax.nn.silu(g)


def setup_inputs(seed: int = 0) -> dict:
    key = jax.random.key(seed)
    ks = jax.random.split(key, 32)
    f32 = jnp.float32

    def nrm(k, shape, scale):
        return jax.random.normal(k, shape, f32) * scale

    def gain(k, shape):
        return 1.0 + 0.05 * jax.random.normal(k, shape, f32)

    n = jnp.arange(S5_STATE, dtype=f32)
    dt0 = jnp.exp(jax.random.uniform(ks[24], (DEPTH, SSD_HEADS), f32, np.log(1e-3), np.log(1e-1)))
    return {
        'x': nrm(ks[0], (BATCH, SEQ, D_MODEL), 1.0),
        'meta_tokens': nrm(ks[1], (N_META, D_MODEL), 1.0),
        'w_in': nrm(ks[2], (DEPTH, D_MODEL, IN_DIM), D_MODEL ** -0.5),
        'w_branch': nrm(ks[3], (DEPTH, N_BRANCH, BRANCH_WIDTH, D_MODEL), BRANCH_WIDTH ** -0.5),
        'w_out': nrm(ks[4], (DEPTH, D_MODEL, D_MODEL), D_MODEL ** -0.5),
        'norm_pre_mix': gain(ks[5], (DEPTH, D_MODEL)),
        'norm_post_mix': gain(ks[6], (DEPTH, D_MODEL)),
        'norm_pre_mlp': gain(ks[7], (DEPTH, D_MODEL)),
        'norm_post_mlp': gain(ks[8], (DEPTH, D_MODEL)),
        'w_up': nrm(ks[9], (DEPTH, D_MODEL, D_FF), D_MODEL ** -0.5),
        'w_down': nrm(ks[10], (DEPTH, D_FF, D_MODEL), D_FF ** -0.5),
        'ret_gn_w': gain(ks[11], (DEPTH, BRANCH_WIDTH)),
        's5_lam_re': -0.5 + 0.01 * jax.random.normal(ks[12], (DEPTH, S5_GROUPS, S5_STATE), f32),
        's5_lam_im': jnp.pi * n + 0.01 * jax.random.normal(ks[13], (DEPTH, S5_GROUPS, S5_STATE), f32),
        's5_b_re': nrm(ks[14], (DEPTH, S5_GROUPS, S5_STATE, S5_GROUP), 1.0),
        's5_b_im': nrm(ks[15], (DEPTH, S5_GROUPS, S5_STATE, S5_GROUP), 1.0),
        's5_c_re': nrm(ks[16], (DEPTH, S5_GROUPS, S5_GROUP, S5_STATE), S5_STATE ** -0.5),
        's5_c_im': nrm(ks[17], (DEPTH, S5_GROUPS, S5_GROUP, S5_STATE), S5_STATE ** -0.5),
        's5_d': nrm(ks[18], (DEPTH, BRANCH_WIDTH), 1.0),
        's5_log_step': jax.random.uniform(ks[19], (DEPTH, S5_GROUPS), f32, np.log(1e-3), np.log(1e-1)),
        's5_glu_w': nrm(ks[20], (DEPTH, BRANCH_WIDTH, 2 * BRANCH_WIDTH), BRANCH_WIDTH ** -0.5),
        's5_glu_b': nrm(ks[21], (DEPTH, 2 * BRANCH_WIDTH), 0.02),
        'ssd_conv_w': nrm(ks[22], (DEPTH, SSD_CONV, SSD_CONV_DIM), SSD_CONV ** -0.5),
        'ssd_conv_b': nrm(ks[23], (DEPTH, SSD_CONV_DIM), 0.02),
        'ssd_dt_bias': dt0 + jnp.log(-jnp.expm1(-dt0)),
        'ssd_a_log': jnp.log(jax.random.uniform(ks[25], (DEPTH, SSD_HEADS), f32, 1.0, 16.0)),
        'ssd_d': gain(ks[26], (DEPTH, SSD_HEADS)),
        'ssd_norm_w': gain(ks[27], (DEPTH, BRANCH_WIDTH)),
        'hgrn_lb': nrm(ks[28], (DEPTH, HG_HEADS * HG_DK), 0.1),
        'hgrn_norm_w': gain(ks[29], (DEPTH, BRANCH_WIDTH)),
    }


def reference(x, meta_tokens, w_in, w_branch, w_out, norm_pre_mix, norm_post_mix, norm_pre_mlp, norm_post_mlp,
              w_up, w_down, ret_gn_w, s5_lam_re, s5_lam_im, s5_b_re, s5_b_im, s5_c_re, s5_c_im, s5_d,
              s5_log_step, s5_glu_w, s5_glu_b, ssd_conv_w, ssd_conv_b, ssd_dt_bias, ssd_a_log, ssd_d,
              ssd_norm_w, hgrn_lb, hgrn_norm_w):
    bsz = x.shape[0]
    meta = jnp.broadcast_to(meta_tokens[None].astype(x.dtype), (bsz, N_META, D_MODEL))
    h = jnp.concatenate([meta, x], axis=1)
    t = h.shape[1]
    lb_all = jnp.cumsum(jax.nn.softmax(hgrn_lb.astype(jnp.float32), axis=0), axis=0)
    lb_all = lb_all - lb_all[0]
    split_at = [int(s) for s in np.cumsum(IN_SIZES)[:-1]]
    for l in range(DEPTH):
        u = rms_norm(h, norm_pre_mix[l])
        proj = u @ w_in[l]
        (rq, rk, rv, rg, s5u, sz, sxbc, sdt, hq, hf, hi, hgate, gates) = jnp.split(proj, split_at, axis=-1)
        y_ret = retention_mixer(rq, rk, rv, rg, ret_gn_w[l])
        y_s5 = s5_mixer(s5u, s5_lam_re[l], s5_lam_im[l], s5_b_re[l], s5_b_im[l], s5_c_re[l], s5_c_im[l],
                        s5_d[l], s5_log_step[l], s5_glu_w[l], s5_glu_b[l])
        y_ssd = ssd_mixer(sz, sxbc, sdt, ssd_conv_w[l], ssd_conv_b[l], ssd_dt_bias[l], ssd_a_log[l],
                          ssd_d[l], ssd_norm_w[l])
        y_hg = hgrn2_mixer(hq, hf, hi, hgate, lb_all[l], hgrn_norm_w[l])
        branches = jnp.stack([y_ret, y_s5, y_ssd, y_hg], axis=2)
        proj_b = jnp.einsum('btnc,ncd->btnd', branches, w_branch[l])
        gate = jax.nn.sigmoid(gates.reshape(bsz, t, N_BRANCH, D_MODEL))
        mixed = jnp.sum(gate * proj_b, axis=2) @ w_out[l]
        h = h + rms_norm(mixed, norm_post_mix[l])
        m = rms_norm(h, norm_pre_mlp[l])
        m = jnp.square(jax.nn.relu(m @ w_up[l])) @ w_down[l]
        h = h + rms_norm(m, norm_post_mlp[l])
    return h[:, N_META:]
```

```python
import functools

import numpy as np
import jax
import jax.numpy as jnp
from jax import lax
from jax.experimental import pallas as pl
from jax.experimental.pallas import tpu as pltpu

F32 = jnp.float32
BF16 = jnp.bfloat16

D_MODEL = 1024
N_META = 16
BRANCH_WIDTH = 512
N_BRANCH = 4
D_FF = 4 * D_MODEL
NORM_EPS = 1e-6
RET_HEADS = 4
RET_DK = 64
RET_DV = 128
RET_ROPE_BASE = 10000.0
S5_GROUP = 16
S5_GROUPS = 32
S5_STATE = 64
SSD_HEADDIM = 64
SSD_HEADS = 8
SSD_GROUPS = 2
SSD_STATE = 128
SSD_CONV = 4
HG_HEADS = 4
HG_DK = 128

NB = 8
PADF = 48
CH = 64
GW = 512
N_GROUPS = 12
NA = GW * N_GROUPS
G_QK, G_RV, G_RG, G_S5, G_Z, G_XS, G_BC, G_HQ, G_HF, G_HI, G_HG, G_DT = range(N_GROUPS)
S5_P2 = S5_GROUPS * S5_STATE
VMEM_LIMIT_CAP = 56 * 1024 * 1024


def _vmem_limit(nbytes):
    return int(min(VMEM_LIMIT_CAP, max(32 * 1024 * 1024, nbytes * 5 // 4 + 8 * 1024 * 1024)))


def _const_spec(shape):
    nd = len(shape)
    return pl.BlockSpec(shape, lambda *a: (0,) * nd, pipeline_mode=pl.Buffered(1))


def _layer_spec(shape, l):
    nd = len(shape)
    return pl.BlockSpec((None,) + tuple(shape), lambda *a: (l,) + (0,) * nd,
                        pipeline_mode=pl.Buffered(1))


def _split_bf16(x):
    hi = x.astype(BF16)
    lo = (x - hi.astype(F32)).astype(BF16)
    return hi, lo


def _dot(a, b):
    return jnp.dot(a, b, preferred_element_type=F32)


def _dot_nt(a, b):
    return lax.dot_general(a, b, (((1,), (1,)), ((), ())), preferred_element_type=F32)


def _dot_tn(a, b):
    return lax.dot_general(a, b, (((0,), (0,)), ((), ())), preferred_element_type=F32)


def _sel_dot(t_bf16, x):
    hi, lo = _split_bf16(x)
    return _dot(t_bf16, hi) + _dot(t_bf16, lo)


def _dot_sel(x, e_bf16):
    hi, lo = _split_bf16(x)
    return _dot(hi, e_bf16) + _dot(lo, e_bf16)


def _sigmoid(x):
    return 1.0 / (1.0 + jnp.exp(-x))


def _silu(x):
    return x * _sigmoid(x)


def _rms(x, w):
    ms = jnp.mean(x * x, axis=-1, keepdims=True)
    return x * lax.rsqrt(ms + NORM_EPS) * w


def _inproj_kernel(h_ref, nw_ref, w_ref, o_ref, xn_ref):
    @pl.when(pl.program_id(1) == 0)
    def _():
        xn_ref[...] = _rms(h_ref[...], nw_ref[...]).astype(BF16)

    o_ref[...] = _dot(xn_ref[...], w_ref[...])


def _inproj(h, nw, w, l, tm, tn):
    rows = h.shape[0]
    nbytes = 2 * (tm * D_MODEL * 4 + D_MODEL * tn * 2 + tm * tn * 4) + tm * D_MODEL * 2
    return pl.pallas_call(
        _inproj_kernel,
        grid=(rows // tm, NA // tn),
        in_specs=[
            pl.BlockSpec((tm, D_MODEL), lambda i, j: (i, 0)),
            pl.BlockSpec((None, 1, D_MODEL), lambda i, j: (l, 0, 0)),
            pl.BlockSpec((None, D_MODEL, tn), lambda i, j: (l, 0, j)),
        ],
        out_specs=pl.BlockSpec((tm, tn), lambda i, j: (i, j)),
        out_shape=jax.ShapeDtypeStruct((rows, NA), F32),
        scratch_shapes=[pltpu.VMEM((tm, D_MODEL), BF16)],
        compiler_params=pltpu.CompilerParams(
            dimension_semantics=("parallel", "arbitrary"), vmem_limit_bytes=_vmem_limit(nbytes)),
        name="inproj",
    )(h, nw, w)


def _ret_tables():
    gam = 1.0 - np.exp2(-5.0 - np.arange(RET_HEADS, dtype=np.float64))
    lg = np.log(gam)
    t = np.arange(CH)
    col_head = (np.arange(2 * 128) % 128) // 32
    dall = np.zeros((CH, RET_HEADS * CH), np.float64)
    for h in range(RET_HEADS):
        d = np.exp(lg[h] * (t[:, None] - t[None, :]))
        dall[:, h * CH:(h + 1) * CH] = np.where(t[:, None] >= t[None, :], d, 0.0)
    qdec = np.exp(lg[col_head][None, :] * (t[:, None] + 1.0))
    kdec = np.exp(lg[col_head][None, :] * (CH - 1.0 - t[:, None]))
    vcol_head = np.arange(RET_HEADS * RET_DV) // RET_DV
    sdec = np.exp(lg[col_head] * CH)[:, None] * np.ones((1, 512))
    smask = (col_head[:, None] == vcol_head[None, :]).astype(np.float64)
    kmask = (col_head[None, :] == np.arange(RET_HEADS)[:, None]).astype(np.float64)[:, None, :]
    vmask = (vcol_head[None, :] == np.arange(RET_HEADS)[:, None]).astype(np.float64)[:, None, :]
    f = lambda a: jnp.asarray(a, F32)
    return f(dall), f(qdec), f(kdec), f(sdec), f(smask), f(kmask), f(vmask)


def _ret_kernel(qk_ref, v_ref, g_ref, cos_ref, sin_ref, dall_ref, qdec_ref, kdec_ref, sdec_ref,
                smask_ref, kmask_ref, vmask_ref, gnw_ref, o_ref, st_ref, *, nch):
    @pl.when(pl.program_id(1) == 0)
    def _():
        st_ref[...] = jnp.zeros_like(st_ref)

    def chunk(c, carry):
        r0 = pl.multiple_of(c * CH, CH)
        rows = pl.ds(r0, CH)
        qk = qk_ref[rows, :]
        cs = cos_ref[rows, :]
        sn = sin_ref[rows, :]
        q1, q2, k1, k2 = qk[:, 0:128], qk[:, 128:256], qk[:, 256:384], qk[:, 384:512]
        q = jnp.concatenate([q1 * cs - q2 * sn, q1 * sn + q2 * cs], axis=1)
        k = jnp.concatenate([k1 * cs - k2 * sn, k1 * sn + k2 * cs], axis=1) * (RET_DK ** -0.5)
        v = v_ref[rows, :]
        kb = k.astype(BF16)
        kstack = jnp.concatenate([kb * kmask_ref[h].astype(BF16) for h in range(RET_HEADS)], axis=0)
        s_all = _dot_nt(q.astype(BF16), kstack)
        p = (s_all * dall_ref[...]).astype(BF16)
        vb = v.astype(BF16)
        vstack = jnp.concatenate([vb * vmask_ref[h].astype(BF16) for h in range(RET_HEADS)], axis=0)
        y = _dot(p, vstack)
        st = st_ref[...]
        y = y + _dot((q * qdec_ref[...]).astype(BF16), st.astype(BF16))
        new = _dot_tn((k * kdec_ref[...]).astype(BF16), vb)
        st_ref[...] = st * sdec_ref[...] + new * smask_ref[...]
        outs = []
        for h in range(RET_HEADS):
            yh = y[:, h * RET_DV:(h + 1) * RET_DV]
            mu = jnp.mean(yh, axis=-1, keepdims=True)
            d = yh - mu
            var = jnp.mean(d * d, axis=-1, keepdims=True)
            outs.append(d * lax.rsqrt(var + NORM_EPS))
        yn = jnp.concatenate(outs, axis=1) * gnw_ref[...]
        o_ref[rows, :] = _silu(g_ref[rows, :]) * yn
        return carry

    lax.fori_loop(0, nch, chunk, 0)


def _ret_call(proj_v, cos_t, sin_t, tabs, gnw, l, tt):
    tp = proj_v.shape[0]
    nch = tt // CH
    grp = lambda g: pl.BlockSpec((tt, GW), lambda b, i, g=g: (i, b * N_GROUPS + g))
    tspec = pl.BlockSpec((tt, 128), lambda b, i: (i, 0))
    dall, qdec, kdec, sdec, smask, kmask, vmask = tabs
    return pl.pallas_call(
        functools.partial(_ret_kernel, nch=nch),
        grid=(NB, tp // tt),
        in_specs=[grp(G_QK), grp(G_RV), grp(G_RG), tspec, tspec,
                  _const_spec(dall.shape), _const_spec(qdec.shape), _const_spec(kdec.shape),
                  _const_spec(sdec.shape), _const_spec(smask.shape), _const_spec(kmask.shape),
                  _const_spec(vmask.shape), _layer_spec((1, GW), l)],
        out_specs=pl.BlockSpec((tt, GW), lambda b, i: (i, b)),
        out_shape=jax.ShapeDtypeStruct((tp, NB * GW), F32),
        scratch_shapes=[pltpu.VMEM((256, 512), F32)],
        compiler_params=pltpu.CompilerParams(
            dimension_semantics=("parallel", "arbitrary"), vmem_limit_bytes=_vmem_limit(24 << 20)),
        name="retention",
    )(proj_v, proj_v, proj_v, cos_t, sin_t, dall, qdec, kdec, sdec, smask, kmask, vmask, gnw)


def _s5_kernel(u_ref, bm_ref, ar_ref, ai_ref, cm_ref, d_ref, gw_ref, gb_ref, o_ref,
               bu_ref, xs_ref, *, tt, cw):
    @pl.when(pl.program_id(0) == 0)
    def _():
        xs_ref[...] = jnp.zeros_like(xs_ref)

    u = u_ref[...]
    bu_ref[...] = _dot(u.astype(BF16), bm_ref[...])
    for c in range(S5_P2 // cw):
        cr = pl.ds(c * cw, cw)
        ci = pl.ds(S5_P2 + c * cw, cw)
        ar = ar_ref[:, cr]
        ai = ai_ref[:, cr]

        def step(t, carry):
            xr, xi = carry
            rows = pl.ds(pl.multiple_of(t * NB, NB), NB)
            nr = ar * xr - ai * xi + bu_ref[rows, cr]
            ni = ar * xi + ai * xr + bu_ref[rows, ci]
            bu_ref[rows, cr] = nr
            bu_ref[rows, ci] = ni
            return nr, ni

        xr, xi = lax.fori_loop(0, tt, step, (xs_ref[:, cr], xs_ref[:, ci]), unroll=4)
        xs_ref[:, cr] = xr
        xs_ref[:, ci] = xi
    y = _dot(bu_ref[...].astype(BF16), cm_ref[...]) + d_ref[...] * u
    y = jax.nn.gelu(y)
    ag = _dot(y.astype(BF16), gw_ref[...]) + gb_ref[...]
    o_ref[...] = ag[:, :BRANCH_WIDTH] * _sigmoid(ag[:, BRANCH_WIDTH:])


def _s5_call(proj, bm, ar, ai, cm, dsk, gw, gb, l, tt):
    rows = proj.shape[0]
    rb = tt * NB
    nbytes = (2 * (rb * GW * 4 * 2) + 2 * (GW * 2 * S5_P2 * 2 * 2 + GW * 2 * GW * 2)
              + rb * 2 * S5_P2 * 4 * 2)
    return pl.pallas_call(
        functools.partial(_s5_kernel, tt=tt, cw=512),
        grid=(rows // rb,),
        in_specs=[pl.BlockSpec((rb, GW), lambda i: (i, G_S5)),
                  _layer_spec((GW, 2 * S5_P2), l), _layer_spec((NB, S5_P2), l),
                  _layer_spec((NB, S5_P2), l), _layer_spec((2 * S5_P2, GW), l),
                  _layer_spec((1, GW), l), _layer_spec((GW, 2 * GW), l), _layer_spec((1, 2 * GW), l)],
        out_specs=pl.BlockSpec((rb, GW), lambda i: (i, 0)),
        out_shape=jax.ShapeDtypeStruct((rows, GW), F32),
        scratch_shapes=[pltpu.VMEM((rb, 2 * S5_P2), F32), pltpu.VMEM((NB, 2 * S5_P2), F32)],
        compiler_params=pltpu.CompilerParams(
            dimension_semantics=("arbitrary",), vmem_limit_bytes=_vmem_limit(nbytes)),
        name="s5",
    )(proj, bm, ar, ai, cm, dsk, gw, gb)


def _ssd_tables():
    t = np.arange(CH)
    ltri = (t[:, None] >= t[None, :]).astype(np.float32)
    expand = np.zeros((128, GW), np.float32)
    for h in range(SSD_HEADS):
        expand[h, h * SSD_HEADDIM:(h + 1) * SSD_HEADDIM] = 1.0
    hm = np.zeros((4, 1, 256), np.float32)
    for h in range(4):
        hm[h, 0, h * SSD_HEADDIM:(h + 1) * SSD_HEADDIM] = 1.0
    return jnp.asarray(ltri, BF16), jnp.asarray(expand, BF16), jnp.asarray(hm, BF16)


def _ssd_kernel(z_ref, xs_ref, bc_ref, dt_ref, ltri_ref, exp_ref, hm_ref, cwx_ref, cwb_ref,
                cbx_ref, cbb_ref, dtb_ref, a_ref, dsk_ref, nw_ref, o_ref,
                st_ref, ext_ref, *, nch, tt):
    @pl.when(pl.program_id(1) == 0)
    def _():
        st_ref[...] = jnp.zeros_like(st_ref)
        ext_ref[...] = jnp.zeros_like(ext_ref)

    t0 = pl.program_id(1) * tt
    tri = lax.broadcasted_iota(jnp.int32, (CH, CH), 0) >= lax.broadcasted_iota(jnp.int32, (CH, CH), 1)
    hpg = SSD_HEADS // SSD_GROUPS

    def chunk(c, carry):
        r0 = pl.multiple_of(c * CH, CH)
        rows = pl.ds(r0, CH)
        ext_ref[pl.ds(NB, CH), 0:GW] = xs_ref[rows, :]
        ext_ref[pl.ds(NB, CH), GW:2 * GW] = bc_ref[rows, :]
        accx = jnp.zeros((CH, GW), F32) + cbx_ref[...]
        accb = jnp.zeros((CH, GW), F32) + cbb_ref[...]
        for j in range(SSD_CONV):
            sl = pl.ds(NB - (SSD_CONV - 1) + j, CH)
            accx = accx + ext_ref[sl, 0:GW] * cwx_ref[j:j + 1, :]
            accb = accb + ext_ref[sl, GW:2 * GW] * cwb_ref[j:j + 1, :]
        ext_ref[0:NB, :] = ext_ref[pl.ds(CH, NB), :]
        valid = ((t0 + r0 + lax.broadcasted_iota(jnp.int32, (CH, 1), 0)) >= PADF).astype(F32)
        xs = _silu(accx) * valid
        bc = _silu(accb)
        dt = jax.nn.softplus(dt_ref[rows, 0:128] + dtb_ref[...])
        la = dt * a_ref[...]
        cum = _sel_dot(ltri_ref[...], la)
        cum_e = _dot_sel(cum, exp_ref[...])
        dt_e = _dot_sel(dt, exp_ref[...])
        last_e = cum_e[CH - 1:CH, :]
        v = xs * dt_e
        vb = v.astype(BF16)
        wv = (v * jnp.exp(last_e - cum_e)).astype(BF16)
        ecum = jnp.exp(cum_e)
        elast = jnp.exp(last_e)
        cum_t = cum.T
        ys = []
        for g in range(SSD_GROUPS):
            gl = slice(g * 256, (g + 1) * 256)
            bm = bc[:, g * 128:(g + 1) * 128].astype(BF16)
            cm = bc[:, 256 + g * 128:256 + (g + 1) * 128].astype(BF16)
            scores = _dot_nt(cm, bm)
            ps = []
            for hh in range(hpg):
                h = g * hpg + hh
                seg = cum[:, h:h + 1] - cum_t[h:h + 1, :]
                ps.append(scores * jnp.exp(jnp.where(tri, seg, -jnp.inf)))
            pcat = jnp.concatenate(ps, axis=1).astype(BF16)
            vg = vb[:, gl]
            vstack = jnp.concatenate([vg * hm_ref[hh] for hh in range(hpg)], axis=0)
            st = st_ref[g]
            yg = _dot(pcat, vstack) + _dot(cm, st.astype(BF16)) * ecum[:, gl]
            st_ref[g] = st * elast[:, gl] + _dot_tn(bm, wv[:, gl])
            ys.append(yg)
        y = jnp.concatenate(ys, axis=1) + dsk_ref[...] * xs
        y = y * _silu(z_ref[rows, :])
        outs = [_rms(y[:, g * 256:(g + 1) * 256], nw_ref[:, g * 256:(g + 1) * 256])
                for g in range(SSD_GROUPS)]
        o_ref[rows, :] = jnp.concatenate(outs, axis=1)
        return carry

    lax.fori_loop(0, nch, chunk, 0)


def _ssd_call(proj_v, tabs, cwx, cwb, cbx, cbb, dtb, a, dsk, nw, l, tt):
    tp = proj_v.shape[0]
    nch = tt // CH
    grp = lambda g: pl.BlockSpec((tt, GW), lambda b, i, g=g: (i, b * N_GROUPS + g))
    ltri, expand, hm = tabs
    return pl.pallas_call(
        functools.partial(_ssd_kernel, nch=nch, tt=tt),
        grid=(NB, tp // tt),
        in_specs=[grp(G_Z), grp(G_XS), grp(G_BC), grp(G_DT),
                  _const_spec(ltri.shape), _const_spec(expand.shape), _const_spec(hm.shape),
                  _layer_spec((SSD_CONV, GW), l), _layer_spec((SSD_CONV, GW), l),
                  _layer_spec((1, GW), l), _layer_spec((1, GW), l), _layer_spec((1, 128), l),
                  _layer_spec((1, 128), l), _layer_spec((1, GW), l), _layer_spec((1, GW), l)],
        out_specs=pl.BlockSpec((tt, GW), lambda b, i: (i, b)),
        out_shape=jax.ShapeDtypeStruct((tp, NB * GW), F32),
        scratch_shapes=[pltpu.VMEM((SSD_GROUPS, SSD_STATE, 256), F32),
                        pltpu.VMEM((CH + 2 * NB, 2 * GW), F32)],
        compiler_params=pltpu.CompilerParams(
            dimension_semantics=("parallel", "arbitrary"), vmem_limit_bytes=_vmem_limit(24 << 20)),
        name="ssd",
    )(proj_v, proj_v, proj_v, proj_v, ltri, expand, hm, cwx, cwb, cbx, cbb, dtb, a, dsk, nw)


HG_LEVELS = (64, 32, 16, 8, 4, 2)


def _hg_tables():
    t = np.arange(CH)
    blocks = []
    for c in HG_LEVELS:
        m = np.zeros((CH, CH), np.float32)
        for r in range(CH):
            start = (r // c) * c
            mid = start + c // 2 - 1
            if r <= mid:
                m[r, r + 1:mid + 1] = 1.0
            else:
                m[r, mid + 1:r + 1] = 1.0
        blocks.append(m)
    blocks.append((t[:, None] >= t[None, :]).astype(np.float32))
    blocks.append((t[None, :] > t[:, None]).astype(np.float32))
    tall = np.concatenate(blocks, axis=0)
    ones_blk = np.kron(np.eye(HG_HEADS, dtype=np.float32), np.ones((HG_DK, HG_DK), np.float32))
    pmask = np.kron(np.eye(2, dtype=np.float32), np.ones((128, 128), np.float32))
    return jnp.asarray(tall, BF16), jnp.asarray(ones_blk, BF16), jnp.asarray(pmask, F32)


def _hg_kernel(q_ref, f_ref, i_ref, g_ref, tall_ref, ones_ref, pmask_ref, lb_ref, nw_ref, o_ref,
               st_ref, *, nch):
    @pl.when(pl.program_id(1) == 0)
    def _():
        st_ref[...] = jnp.zeros_like(st_ref)

    row = lax.broadcasted_iota(jnp.int32, (CH, 1), 0)
    trow = lax.broadcasted_iota(jnp.int32, (CH, 128), 0)
    scol = lax.broadcasted_iota(jnp.int32, (CH, 128), 1) % CH
    zero128 = jnp.zeros((CH, 128), BF16)

    def chunk(c, carry):
        r0 = pl.multiple_of(c * CH, CH)
        rows = pl.ds(r0, CH)
        lb = lb_ref[...]
        f = lb + (1.0 - lb) * _sigmoid(f_ref[rows, :])
        logf = jnp.log(f)
        k = 1.0 - f
        q = _silu(q_ref[rows, :])
        v = i_ref[rows, :]
        vb = v.astype(BF16)
        zall = _sel_dot(tall_ref[...], logf)
        nl = len(HG_LEVELS)
        cum = zall[nl * CH:(nl + 1) * CH, :]
        rem = zall[(nl + 1) * CH:(nl + 2) * CH, :]
        acc = [jnp.zeros((CH, 128), F32), jnp.zeros((CH, 128), F32)]
        for li, cs in enumerate(HG_LEVELS):
            fh = (row % cs) < (cs // 2)
            x = jnp.where(fh, k, q) * jnp.exp(zall[li * CH:(li + 1) * CH, :])
            qt = jnp.where(fh, 0.0, x).astype(BF16)
            kt = jnp.where(fh, x, 0.0).astype(BF16)
            lmask = (((trow // cs) == (scol // cs)) & ((trow % cs) >= (cs // 2))
                     & ((scol % cs) < (cs // 2))).astype(F32)
            for p in range(2):
                a0 = kt[:, (2 * p) * 128:(2 * p + 1) * 128]
                a1 = kt[:, (2 * p + 1) * 128:(2 * p + 2) * 128]
                kstack = jnp.concatenate([jnp.concatenate([a0, zero128], axis=1),
                                          jnp.concatenate([zero128, a1], axis=1)], axis=0)
                s = _dot_nt(qt[:, p * 256:(p + 1) * 256], kstack)
                acc[p] = acc[p] + s * lmask
        r_same = _dot((q * k).astype(BF16), ones_ref[...])
        qd = (q * jnp.exp(cum)).astype(BF16)
        kw = (k * jnp.exp(rem)).astype(BF16)
        elast = jnp.exp(cum[CH - 1:CH, :])
        ys = []
        for p in range(2):
            pl_ = slice(p * 256, (p + 1) * 256)
            v0 = vb[:, (2 * p) * 128:(2 * p + 1) * 128]
            v1 = vb[:, (2 * p + 1) * 128:(2 * p + 2) * 128]
            vstack = jnp.concatenate([jnp.concatenate([v0, zero128], axis=1),
                                      jnp.concatenate([zero128, v1], axis=1)], axis=0)
            st = st_ref[p]
            yp = _dot(acc[p].astype(BF16), vstack) + _dot_nt(qd[:, pl_], st.astype(BF16))
            st_ref[p] = st * elast[:, pl_] + _dot_tn(vb[:, pl_], kw[:, pl_]) * pmask_ref[...]
            ys.append(yp)
        o = jnp.concatenate(ys, axis=1) + r_same * v
        outs = [_rms(o[:, h * 128:(h + 1) * 128], nw_ref[:, h * 128:(h + 1) * 128])
                for h in range(HG_HEADS)]
        o_ref[rows, :] = jnp.concatenate(outs, axis=1) * _silu(g_ref[rows, :])
        return carry

    lax.fori_loop(0, nch, chunk, 0)


def _hg_call(proj_v, tabs, lb, nw, l, tt):
    tp = proj_v.shape[0]
    nch = tt // CH
    grp = lambda g: pl.BlockSpec((tt, GW), lambda b, i, g=g: (i, b * N_GROUPS + g))
    tall, ones_blk, pmask = tabs
    return pl.pallas_call(
        functools.partial(_hg_kernel, nch=nch),
        grid=(NB, tp // tt),
        in_specs=[grp(G_HQ), grp(G_HF), grp(G_HI), grp(G_HG),
                  _const_spec(tall.shape), _const_spec(ones_blk.shape), _const_spec(pmask.shape),
                  _layer_spec((1, GW), l), _layer_spec((1, GW), l)],
        out_specs=pl.BlockSpec((tt, GW), lambda b, i: (i, b)),
        out_shape=jax.ShapeDtypeStruct((tp, NB * GW), F32),
        scratch_shapes=[pltpu.VMEM((2, 256, 256), F32)],
        compiler_params=pltpu.CompilerParams(
            dimension_semantics=("parallel", "arbitrary"), vmem_limit_bytes=_vmem_limit(24 << 20)),
        name="hgrn2",
    )(proj_v, proj_v, proj_v, proj_v, tall, ones_blk, pmask, lb, nw)


def _merge_kernel(h_ref, y0_ref, y1_ref, y2_ref, y3_ref, npre_ref, wg_ref, wb_ref, wo_ref,
                  npost_ref, o_ref, *, tm):
    h = h_ref[...]
    u = _rms(h, npre_ref[...]).astype(BF16)
    mixed = jnp.zeros((tm, D_MODEL), F32)
    for n, y_ref in enumerate((y0_ref, y1_ref, y2_ref, y3_ref)):
        gate = _sigmoid(_dot(u, wg_ref[:, n * D_MODEL:(n + 1) * D_MODEL]))
        mixed = mixed + gate * _dot(y_ref[...].astype(BF16), wb_ref[n])
    out = _dot(mixed.astype(BF16), wo_ref[...])
    row = pl.program_id(0) * tm + lax.broadcasted_iota(jnp.int32, (tm, 1), 0)
    o_ref[...] = jnp.where(row >= PADF * NB, h + _rms(out, npost_ref[...]), 0.0)


def _merge_call(h, ys, npre, wg, wb, wo, npost, l, tm):
    rows = h.shape[0]
    nbytes = (2 * (2 * tm * D_MODEL * 4 + 4 * tm * GW * 4)
              + 2 * (D_MODEL * 4 * D_MODEL + 4 * GW * D_MODEL + D_MODEL * D_MODEL) * 2
              + 6 * tm * D_MODEL * 4)
    yspec = pl.BlockSpec((tm, GW), lambda i: (i, 0))
    return pl.pallas_call(
        functools.partial(_merge_kernel, tm=tm),
        grid=(rows // tm,),
        in_specs=[pl.BlockSpec((tm, D_MODEL), lambda i: (i, 0)), yspec, yspec, yspec, yspec,
                  _layer_spec((1, D_MODEL), l), _layer_spec((D_MODEL, 4 * D_MODEL), l),
                  _layer_spec((N_BRANCH, GW, D_MODEL), l), _layer_spec((D_MODEL, D_MODEL), l),
                  _layer_spec((1, D_MODEL), l)],
        out_specs=pl.BlockSpec((tm, D_MODEL), lambda i: (i, 0)),
        out_shape=jax.ShapeDtypeStruct((rows, D_MODEL), F32),
        compiler_params=pltpu.CompilerParams(
            dimension_semantics=("parallel",), vmem_limit_bytes=_vmem_limit(nbytes)),
        name="merge",
    )(h, *ys, npre, wg, wb, wo, npost)


def _mlp_kernel(h_ref, npre_ref, wu_ref, wd_ref, npost_ref, o_ref, *, tm, fc):
    h = h_ref[...]
    m = _rms(h, npre_ref[...]).astype(BF16)
    acc = jnp.zeros((tm, D_MODEL), F32)
    for c in range(D_FF // fc):
        a = jnp.maximum(_dot(m, wu_ref[:, c * fc:(c + 1) * fc]), 0.0)
        acc = acc + _dot((a * a).astype(BF16), wd_ref[c * fc:(c + 1) * fc, :])
    row = pl.program_id(0) * tm + lax.broadcasted_iota(jnp.int32, (tm, 1), 0)
    o_ref[...] = jnp.where(row >= PADF * NB, h + _rms(acc, npost_ref[...]), 0.0)


def _mlp_call(h, npre, wu, wd, npost, l, tm):
    rows = h.shape[0]
    nbytes = 2 * (2 * tm * D_MODEL * 4) + 2 * (2 * D_MODEL * D_FF * 2) + 6 * tm * D_MODEL * 4
    return pl.pallas_call(
        functools.partial(_mlp_kernel, tm=tm, fc=1024),
        grid=(rows // tm,),
        in_specs=[pl.BlockSpec((tm, D_MODEL), lambda i: (i, 0)), _layer_spec((1, D_MODEL), l),
                  _layer_spec((D_MODEL, D_FF), l), _layer_spec((D_FF, D_MODEL), l),
                  _layer_spec((1, D_MODEL), l)],
        out_specs=pl.BlockSpec((tm, D_MODEL), lambda i: (i, 0)),
        out_shape=jax.ShapeDtypeStruct((rows, D_MODEL), F32),
        compiler_params=pltpu.CompilerParams(
            dimension_semantics=("parallel",), vmem_limit_bytes=_vmem_limit(nbytes)),
        name="mlp",
    )(h, npre, wu, wd, npost)


def _prep_w_in(w_in):
    depth = w_in.shape[0]
    perm = np.zeros(256, np.int64)
    for half in range(2):
        for h in range(RET_HEADS):
            for i in range(32):
                perm[half * 128 + h * 32 + i] = h * 64 + half * 32 + i
    rq = w_in[:, :, 0:256][:, :, perm]
    rk = w_in[:, :, 256:512][:, :, perm]
    rest = w_in[:, :, 512:3584]
    hg = w_in[:, :, 3592:5640]
    dt = jnp.pad(w_in[:, :, 3584:3592], ((0, 0), (0, 0), (0, GW - SSD_HEADS)))
    wa = jnp.concatenate([rq, rk, rest, hg, dt], axis=2).astype(BF16)
    wg = w_in[:, :, 5640:].astype(BF16)
    assert wa.shape == (depth, D_MODEL, NA)
    return wa, wg


def _prep_s5(lam_re, lam_im, b_re, b_im, c_re, c_im, log_step):
    depth = lam_re.shape[0]
    step = jnp.exp(log_step.astype(F32))[..., None]
    lr, li = lam_re.astype(F32), lam_im.astype(F32)
    mag = jnp.exp(lr * step)
    ab_re, ab_im = mag * jnp.cos(li * step), mag * jnp.sin(li * step)
    inv = 1.0 / (lr * lr + li * li)
    co_re = ((ab_re - 1.0) * lr + ab_im * li) * inv
    co_im = (ab_im * lr - (ab_re - 1.0) * li) * inv
    br, bi = b_re.astype(F32), b_im.astype(F32)
    bb_re = co_re[..., None] * br - co_im[..., None] * bi
    bb_im = co_re[..., None] * bi + co_im[..., None] * br
    eye = jnp.eye(S5_GROUPS, dtype=F32)
    bd = lambda m: jnp.einsum('lgpj,gh->lgjhp', m, eye).reshape(depth, GW, S5_P2)
    bm = jnp.concatenate([bd(bb_re), bd(bb_im)], axis=2).astype(BF16)
    cd = lambda m: jnp.einsum('lgjp,gh->lgphj', m, eye).reshape(depth, S5_P2, GW)
    cm = jnp.concatenate([cd(c_re.astype(F32)), -cd(c_im.astype(F32))], axis=1).astype(BF16)
    ar = jnp.broadcast_to(ab_re.reshape(depth, 1, S5_P2), (depth, NB, S5_P2))
    ai = jnp.broadcast_to(ab_im.reshape(depth, 1, S5_P2), (depth, NB, S5_P2))
    return bm, ar, ai, cm


def _rope_tables(tp):
    half = RET_DK // 2
    inv_freq = RET_ROPE_BASE ** (-jnp.arange(half, dtype=F32) / half)
    pos = jnp.maximum(jnp.arange(tp, dtype=F32) - PADF, 0.0)
    ang = pos[:, None] * inv_freq[None, :]
    return jnp.tile(jnp.cos(ang), (1, RET_HEADS)), jnp.tile(jnp.sin(ang), (1, RET_HEADS))


def _pick_tiles(tp):
    tt = 192 if tp % 192 == 0 else CH
    rows = tp * NB
    tm = 512
    tm_in = 1536 if rows % 1536 == 0 else 512
    return tt, tm, tm_in


def kernel(x, meta_tokens, w_in, w_branch, w_out, norm_pre_mix, norm_post_mix, norm_pre_mlp, norm_post_mlp, w_up, w_down, ret_gn_w, s5_lam_re, s5_lam_im, s5_b_re, s5_b_im, s5_c_re, s5_c_im, s5_d, s5_log_step, s5_glu_w, s5_glu_b, ssd_conv_w, ssd_conv_b, ssd_dt_bias, ssd_a_log, ssd_d, ssd_norm_w, hgrn_lb, hgrn_norm_w):
    bsz, seq, _ = x.shape
    depth = w_in.shape[0]
    assert bsz == NB and x.shape[2] == D_MODEL
    t_real = N_META + seq
    tp = PADF + t_real
    assert tp % CH == 0
    tt, tm, tm_in = _pick_tiles(tp)
    rows = tp * NB

    meta = jnp.broadcast_to(meta_tokens[None].astype(x.dtype), (bsz, N_META, D_MODEL))
    h = jnp.concatenate([jnp.zeros((bsz, PADF, D_MODEL), x.dtype), meta, x], axis=1)
    h = jnp.transpose(h, (1, 0, 2)).reshape(rows, D_MODEL).astype(F32)

    wa, wg = _prep_w_in(w_in)
    wb = w_branch.astype(BF16)
    wo = w_out.astype(BF16)
    wu = w_up.astype(BF16)
    wd = w_down.astype(BF16)
    r3 = lambda a: a.astype(F32).reshape(depth, 1, -1)
    npre, npost, mpre, mpost = r3(norm_pre_mix), r3(norm_post_mix), r3(norm_pre_mlp), r3(norm_post_mlp)
    bm, ar, ai, cm = _prep_s5(s5_lam_re, s5_lam_im, s5_b_re, s5_b_im, s5_c_re, s5_c_im, s5_log_step)
    s5gw = s5_glu_w.astype(BF16)
    s5gb = r3(s5_glu_b)
    s5d = r3(s5_d)
    cwx = ssd_conv_w[:, :, :GW].astype(F32)
    cwb = ssd_conv_w[:, :, GW:].astype(F32)
    cbx = r3(ssd_conv_b[:, :GW])
    cbb = r3(ssd_conv_b[:, GW:])
    pad8 = lambda a: jnp.pad(a.astype(F32), ((0, 0), (0, 128 - SSD_HEADS))).reshape(depth, 1, 128)
    dtb = pad8(ssd_dt_bias)
    ssd_a = pad8(-jnp.exp(ssd_a_log.astype(F32)))
    ssd_dsk = r3(jnp.repeat(ssd_d.astype(F32), SSD_HEADDIM, axis=1))
    ssd_nw = r3(ssd_norm_w)
    lb_all = jnp.cumsum(jax.nn.softmax(hgrn_lb.astype(F32), axis=0), axis=0)
    lb_all = r3(lb_all - lb_all[0])
    hg_nw = r3(hgrn_norm_w)
    gnw = r3(ret_gn_w)

    cos_t, sin_t = _rope_tables(tp)
    ret_tabs, ssd_tabs, hg_tabs = _ret_tables(), _ssd_tables(), _hg_tables()

    for l in range(depth):
        proj = _inproj(h, npre, wa, l, tm_in, 1536)
        proj_v = proj.reshape(tp, NB * NA)
        y_ret = _ret_call(proj_v, cos_t, sin_t, ret_tabs, gnw, l, tt).reshape(rows, GW)
        y_s5 = _s5_call(proj, bm, ar, ai, cm, s5d, s5gw, s5gb, l, CH)
        y_ssd = _ssd_call(proj_v, ssd_tabs, cwx, cwb, cbx, cbb, dtb, ssd_a, ssd_dsk, ssd_nw, l, tt)
        y_hg = _hg_call(proj_v, hg_tabs, lb_all, hg_nw, l, tt).reshape(rows, GW)
        h = _merge_call(h, (y_ret, y_s5, y_ssd.reshape(rows, GW), y_hg), npre, wg, wb, wo, npost, l, tm)
        h = _mlp_call(h, mpre, wu, wd, mpost, l, tm)

    out = h.reshape(tp, NB, D_MODEL)[PADF + N_META:]
    return jnp.transpose(out, (1, 0, 2)).astype(x.dtype)
```

```python
import functools

import numpy as np
import jax
import jax.numpy as jnp
from jax import lax
from jax.experimental import pallas as pl
from jax.experimental.pallas import tpu as pltpu

F32 = jnp.float32
BF16 = jnp.bfloat16

D_MODEL = 1024
N_META = 16
BRANCH_WIDTH = 512
N_BRANCH = 4
D_FF = 4 * D_MODEL
NORM_EPS = 1e-6
RET_HEADS = 4
RET_DK = 64
RET_DV = 128
RET_ROPE_BASE = 10000.0
S5_GROUP = 16
S5_GROUPS = 32
S5_STATE = 64
SSD_HEADDIM = 64
SSD_HEADS = 8
SSD_GROUPS = 2
SSD_STATE = 128
SSD_CONV = 4
HG_HEADS = 4
HG_DK = 128

NB = 8
PADF = 48
CH = 64
GW = 512
N_GROUPS = 12
NA = GW * N_GROUPS
G_QK, G_RV, G_RG, G_S5, G_Z, G_XS, G_BC, G_HQ, G_HF, G_HI, G_HG, G_DT = range(N_GROUPS)
S5_P2 = S5_GROUPS * S5_STATE
S5_SG = 4
S5_SGW = 2 * S5_P2 // S5_SG
MXU_T = 256
NBP = 2
VMEM_LIMIT_CAP = 56 * 1024 * 1024


def _vmem_limit(nbytes):
    return int(min(VMEM_LIMIT_CAP, max(32 * 1024 * 1024, nbytes * 5 // 4 + 8 * 1024 * 1024)))


def _const_spec(shape):
    nd = len(shape)
    return pl.BlockSpec(shape, lambda *a: (0,) * nd, pipeline_mode=pl.Buffered(1))


def _layer_spec(shape, l):
    nd = len(shape)
    return pl.BlockSpec((None,) + tuple(shape), lambda *a: (l,) + (0,) * nd,
                        pipeline_mode=pl.Buffered(1))


def _split_bf16(x):
    hi = x.astype(BF16)
    lo = (x - hi.astype(F32)).astype(BF16)
    return hi, lo


def _dot(a, b):
    return jnp.dot(a, b, preferred_element_type=F32)


def _dot_nt(a, b):
    return lax.dot_general(a, b, (((1,), (1,)), ((), ())), preferred_element_type=F32)


def _dot_tn(a, b):
    return lax.dot_general(a, b, (((0,), (0,)), ((), ())), preferred_element_type=F32)


def _sel_dot(t_bf16, x):
    hi, lo = _split_bf16(x)
    return _dot(t_bf16, hi) + _dot(t_bf16, lo)


def _dot_sel(x, e_bf16):
    hi, lo = _split_bf16(x)
    return _dot(hi, e_bf16) + _dot(lo, e_bf16)


def _sigmoid(x):
    return 1.0 / (1.0 + jnp.exp(-x))


def _silu(x):
    return x * _sigmoid(x)


def _rms(x, w):
    ms = jnp.mean(x * x, axis=-1, keepdims=True)
    return x * lax.rsqrt(ms + NORM_EPS) * w


def _valid_rows(tile_idx, tiles_per_seq, tm):
    tpos = (tile_idx % tiles_per_seq) * tm + lax.broadcasted_iota(jnp.int32, (tm, 1), 0)
    return tpos >= PADF


def _inproj_kernel(h_ref, nw_ref, w_ref, o_ref, xn_ref):
    @pl.when(pl.program_id(1) == 0)
    def _():
        xn_ref[...] = _rms(h_ref[...], nw_ref[...]).astype(BF16)

    o_ref[...] = _dot(xn_ref[...], w_ref[...])


def _inproj(h, nw, w, l, tm, tn):
    rows = h.shape[0]
    nbytes = 2 * (tm * D_MODEL * 4 + D_MODEL * tn * 2 + tm * tn * 4) + tm * D_MODEL * 2
    return pl.pallas_call(
        _inproj_kernel,
        grid=(rows // tm, NA // tn),
        in_specs=[
            pl.BlockSpec((tm, D_MODEL), lambda i, j: (i, 0)),
            pl.BlockSpec((None, 1, D_MODEL), lambda i, j: (l, 0, 0)),
            pl.BlockSpec((None, D_MODEL, tn), lambda i, j: (l, 0, j)),
        ],
        out_specs=pl.BlockSpec((tm, tn), lambda i, j: (i, j)),
        out_shape=jax.ShapeDtypeStruct((rows, NA), F32),
        scratch_shapes=[pltpu.VMEM((tm, D_MODEL), BF16)],
        compiler_params=pltpu.CompilerParams(
            dimension_semantics=("parallel", "arbitrary"), vmem_limit_bytes=_vmem_limit(nbytes)),
        name="inproj",
    )(h, nw, w)


def _mixer_grp_spec(tt, g):
    return pl.BlockSpec((NBP, tt, GW), lambda bb, i, g=g: (bb, i, g))


def _mixer_params(est_bytes):
    return pltpu.CompilerParams(dimension_semantics=("parallel", "arbitrary"),
                                vmem_limit_bytes=_vmem_limit(est_bytes))


def _ret_tables():
    gam = 1.0 - np.exp2(-5.0 - np.arange(RET_HEADS, dtype=np.float64))
    lg = np.log(gam)
    t = np.arange(CH)
    col_head = (np.arange(2 * 128) % 128) // 32
    dall = np.zeros((CH, RET_HEADS * CH), np.float64)
    for h in range(RET_HEADS):
        d = np.exp(lg[h] * (t[:, None] - t[None, :]))
        dall[:, h * CH:(h + 1) * CH] = np.where(t[:, None] >= t[None, :], d, 0.0)
    qdec = np.exp(lg[col_head][None, :] * (t[:, None] + 1.0))
    kdec = np.exp(lg[col_head][None, :] * (CH - 1.0 - t[:, None]))
    vcol_head = np.arange(RET_HEADS * RET_DV) // RET_DV
    sdec = np.exp(lg[col_head] * CH)[:, None] * np.ones((1, 512))
    smask = (col_head[:, None] == vcol_head[None, :]).astype(np.float64)
    kmask = (col_head[None, :] == np.arange(RET_HEADS)[:, None]).astype(np.float64)[:, None, :]
    vmask = (vcol_head[None, :] == np.arange(RET_HEADS)[:, None]).astype(np.float64)[:, None, :]
    f = lambda a: jnp.asarray(a, F32)
    return f(dall), f(qdec), f(kdec), f(sdec), f(smask), f(kmask), f(vmask)


def _ret_kernel(qk_ref, v_ref, g_ref, cos_ref, sin_ref, dall_ref, qdec_ref, kdec_ref, sdec_ref,
                smask_ref, kmask_ref, vmask_ref, gnw_ref, o_ref, st_ref, *, nch):
    @pl.when(pl.program_id(1) == 0)
    def _():
        st_ref[...] = jnp.zeros_like(st_ref)

    def one(s, rows):
        qk = qk_ref[s, rows, :]
        cs = cos_ref[rows, :]
        sn = sin_ref[rows, :]
        q1, q2, k1, k2 = qk[:, 0:128], qk[:, 128:256], qk[:, 256:384], qk[:, 384:512]
        q = jnp.concatenate([q1 * cs - q2 * sn, q1 * sn + q2 * cs], axis=1)
        k = jnp.concatenate([k1 * cs - k2 * sn, k1 * sn + k2 * cs], axis=1) * (RET_DK ** -0.5)
        v = v_ref[s, rows, :]
        kb = k.astype(BF16)
        kstack = jnp.concatenate([kb * kmask_ref[h].astype(BF16) for h in range(RET_HEADS)], axis=0)
        s_all = _dot_nt(q.astype(BF16), kstack)
        p = (s_all * dall_ref[...]).astype(BF16)
        vb = v.astype(BF16)
        vstack = jnp.concatenate([vb * vmask_ref[h].astype(BF16) for h in range(RET_HEADS)], axis=0)
        y = _dot(p, vstack)
        st = st_ref[s]
        y = y + _dot((q * qdec_ref[...]).astype(BF16), st.astype(BF16))
        new = _dot_tn((k * kdec_ref[...]).astype(BF16), vb)
        st_ref[s] = st * sdec_ref[...] + new * smask_ref[...]
        outs = []
        for h in range(RET_HEADS):
            yh = y[:, h * RET_DV:(h + 1) * RET_DV]
            mu = jnp.mean(yh, axis=-1, keepdims=True)
            d = yh - mu
            var = jnp.mean(d * d, axis=-1, keepdims=True)
            outs.append(d * lax.rsqrt(var + NORM_EPS))
        yn = jnp.concatenate(outs, axis=1) * gnw_ref[...]
        o_ref[s, rows, :] = _silu(g_ref[s, rows, :]) * yn

    def chunk(c, carry):
        rows = pl.ds(pl.multiple_of(c * CH, CH), CH)
        for s in range(NBP):
            one(s, rows)
        return carry

    lax.fori_loop(0, nch, chunk, 0)


def _ret_call(proj3, cos_t, sin_t, tabs, gnw, l, tt):
    tp = proj3.shape[1]
    tspec = pl.BlockSpec((tt, 128), lambda bb, i: (i, 0))
    dall, qdec, kdec, sdec, smask, kmask, vmask = tabs
    return pl.pallas_call(
        functools.partial(_ret_kernel, nch=tt // CH),
        grid=(NB // NBP, tp // tt),
        in_specs=[_mixer_grp_spec(tt, G_QK), _mixer_grp_spec(tt, G_RV), _mixer_grp_spec(tt, G_RG),
                  tspec, tspec,
                  _const_spec(dall.shape), _const_spec(qdec.shape), _const_spec(kdec.shape),
                  _const_spec(sdec.shape), _const_spec(smask.shape), _const_spec(kmask.shape),
                  _const_spec(vmask.shape), _layer_spec((1, GW), l)],
        out_specs=pl.BlockSpec((NBP, tt, GW), lambda bb, i: (bb, i, 0)),
        out_shape=jax.ShapeDtypeStruct((NB, tp, GW), F32),
        scratch_shapes=[pltpu.VMEM((NBP, 256, 512), F32)],
        compiler_params=_mixer_params(2 * 4 * NBP * tt * GW * 4 + (4 << 20)),
        name="retention",
    )(proj3, proj3, proj3, cos_t, sin_t, dall, qdec, kdec, sdec, smask, kmask, vmask, gnw)


def _s5_kernel(u_ref, bm_ref, ar_ref, ai_ref, cm_ref, d_ref, gw_ref, gb_ref, o_ref,
               utm_ref, ytm_ref, bu_ref, xs_ref, *, tt):
    @pl.when(pl.program_id(0) == 0)
    def _():
        xs_ref[...] = jnp.zeros_like(xs_ref)

    for b in range(NB):
        utm_ref[:, b, :] = u_ref[b]
    u = utm_ref[...].reshape(tt * NB, GW)
    ub = u.astype(BF16)
    half = S5_SGW // 2
    for nt in range(2 * S5_P2 // MXU_T):
        kb = nt // (2 * S5_P2 // MXU_T // (GW // MXU_T))
        bu_ref[:, nt * MXU_T:(nt + 1) * MXU_T] = _dot(ub[:, kb * MXU_T:(kb + 1) * MXU_T], bm_ref[nt])
    for c in range(S5_SG):
        cr = pl.ds(c * S5_SGW, half)
        ci = pl.ds(c * S5_SGW + half, half)
        ar = ar_ref[:, pl.ds(c * half, half)]
        ai = ai_ref[:, pl.ds(c * half, half)]

        def step(t, carry):
            xr, xi = carry
            rows = pl.ds(pl.multiple_of(t * NB, NB), NB)
            nr = ar * xr - ai * xi + bu_ref[rows, cr]
            ni = ar * xi + ai * xr + bu_ref[rows, ci]
            bu_ref[rows, cr] = nr
            bu_ref[rows, ci] = ni
            return nr, ni

        xr, xi = lax.fori_loop(0, tt, step, (xs_ref[:, cr], xs_ref[:, ci]), unroll=4)
        xs_ref[:, cr] = xr
        xs_ref[:, ci] = xi
    ys = []
    for j in range(GW // MXU_T):
        acc = None
        for c in range(j * (S5_SG // 2), (j + 1) * (S5_SG // 2)):
            part = _dot(bu_ref[:, c * S5_SGW:(c + 1) * S5_SGW].astype(BF16), cm_ref[c])
            acc = part if acc is None else acc + part
        ys.append(acc)
    y = jnp.concatenate(ys, axis=1) + d_ref[...] * u
    y = jax.nn.gelu(y)
    ag = _dot(y.astype(BF16), gw_ref[...]) + gb_ref[...]
    ytm_ref[...] = (ag[:, :BRANCH_WIDTH] * _sigmoid(ag[:, BRANCH_WIDTH:])).reshape(tt, NB, GW)
    for b in range(NB):
        o_ref[b] = ytm_ref[:, b, :]


def _s5_call(proj3, bm, ar, ai, cm, dsk, gw, gb, l, tt):
    tp = proj3.shape[1]
    rb = tt * NB
    nbytes = (4 * rb * GW * 4 + 2 * rb * GW * 4 + rb * 2 * S5_P2 * 4 * 2
              + (bm.shape[1] * MXU_T * MXU_T + cm.shape[1] * S5_SGW * MXU_T + GW * 2 * GW) * 2)
    return pl.pallas_call(
        functools.partial(_s5_kernel, tt=tt),
        grid=(tp // tt,),
        in_specs=[pl.BlockSpec((NB, tt, GW), lambda i: (0, i, G_S5)),
                  _layer_spec(bm.shape[1:], l), _layer_spec((NB, S5_P2), l),
                  _layer_spec((NB, S5_P2), l), _layer_spec(cm.shape[1:], l),
                  _layer_spec((1, GW), l), _layer_spec((GW, 2 * GW), l), _layer_spec((1, 2 * GW), l)],
        out_specs=pl.BlockSpec((NB, tt, GW), lambda i: (0, i, 0)),
        out_shape=jax.ShapeDtypeStruct((NB, tp, GW), F32),
        scratch_shapes=[pltpu.VMEM((tt, NB, GW), F32), pltpu.VMEM((tt, NB, GW), F32),
                        pltpu.VMEM((rb, 2 * S5_P2), F32), pltpu.VMEM((NB, 2 * S5_P2), F32)],
        compiler_params=pltpu.CompilerParams(
            dimension_semantics=("arbitrary",), vmem_limit_bytes=_vmem_limit(nbytes)),
        name="s5",
    )(proj3, bm, ar, ai, cm, dsk, gw, gb)


def _ssd_tables():
    t = np.arange(CH)
    ltri = (t[:, None] >= t[None, :]).astype(np.float32)
    expand = np.zeros((128, GW), np.float32)
    for h in range(SSD_HEADS):
        expand[h, h * SSD_HEADDIM:(h + 1) * SSD_HEADDIM] = 1.0
    hm = np.zeros((4, 1, 256), np.float32)
    for h in range(4):
        hm[h, 0, h * SSD_HEADDIM:(h + 1) * SSD_HEADDIM] = 1.0
    return jnp.asarray(ltri, BF16), jnp.asarray(expand, BF16), jnp.asarray(hm, BF16)


def _ssd_kernel(z_ref, xs_ref, bc_ref, dt_ref, ltri_ref, exp_ref, hm_ref, cwx_ref, cwb_ref,
                cbx_ref, cbb_ref, dtb_ref, a_ref, dsk_ref, nw_ref, o_ref,
                st_ref, ext_ref, *, nch, tt):
    @pl.when(pl.program_id(1) == 0)
    def _():
        st_ref[...] = jnp.zeros_like(st_ref)
        ext_ref[...] = jnp.zeros_like(ext_ref)

    t0 = pl.program_id(1) * tt
    tri = lax.broadcasted_iota(jnp.int32, (CH, CH), 0) >= lax.broadcasted_iota(jnp.int32, (CH, CH), 1)
    hpg = SSD_HEADS // SSD_GROUPS

    def one(s, r0, rows):
        ext_ref[s, pl.ds(NB, CH), 0:GW] = xs_ref[s, rows, :]
        ext_ref[s, pl.ds(NB, CH), GW:2 * GW] = bc_ref[s, rows, :]
        accx = jnp.zeros((CH, GW), F32) + cbx_ref[...]
        accb = jnp.zeros((CH, GW), F32) + cbb_ref[...]
        for j in range(SSD_CONV):
            sl = pl.ds(NB - (SSD_CONV - 1) + j, CH)
            accx = accx + ext_ref[s, sl, 0:GW] * cwx_ref[j:j + 1, :]
            accb = accb + ext_ref[s, sl, GW:2 * GW] * cwb_ref[j:j + 1, :]
        ext_ref[s, 0:NB, :] = ext_ref[s, pl.ds(CH, NB), :]
        valid = ((t0 + r0 + lax.broadcasted_iota(jnp.int32, (CH, 1), 0)) >= PADF).astype(F32)
        xs = _silu(accx) * valid
        bc = _silu(accb)
        dt = jax.nn.softplus(dt_ref[s, rows, 0:128] + dtb_ref[...])
        la = dt * a_ref[...]
        cum = _sel_dot(ltri_ref[...], la)
        cum_e = _dot_sel(cum, exp_ref[...])
        dt_e = _dot_sel(dt, exp_ref[...])
        last_e = cum_e[CH - 1:CH, :]
        v = xs * dt_e
        vb = v.astype(BF16)
        wv = (v * jnp.exp(last_e - cum_e)).astype(BF16)
        ecum = jnp.exp(cum_e)
        elast = jnp.exp(last_e)
        cum_t = cum.T
        ys = []
        for g in range(SSD_GROUPS):
            gl = slice(g * 256, (g + 1) * 256)
            bm = bc[:, g * 128:(g + 1) * 128].astype(BF16)
            cm = bc[:, 256 + g * 128:256 + (g + 1) * 128].astype(BF16)
            scores = _dot_nt(cm, bm)
            ps = []
            for hh in range(hpg):
                h = g * hpg + hh
                seg = cum[:, h:h + 1] - cum_t[h:h + 1, :]
                ps.append(scores * jnp.exp(jnp.where(tri, seg, -jnp.inf)))
            pcat = jnp.concatenate(ps, axis=1).astype(BF16)
            vg = vb[:, gl]
            vstack = jnp.concatenate([vg * hm_ref[hh] for hh in range(hpg)], axis=0)
            st = st_ref[s, g]
            yg = _dot(pcat, vstack) + _dot(cm, st.astype(BF16)) * ecum[:, gl]
            st_ref[s, g] = st * elast[:, gl] + _dot_tn(bm, wv[:, gl])
            ys.append(yg)
        y = jnp.concatenate(ys, axis=1) + dsk_ref[...] * xs
        y = y * _silu(z_ref[s, rows, :])
        outs = [_rms(y[:, g * 256:(g + 1) * 256], nw_ref[:, g * 256:(g + 1) * 256])
                for g in range(SSD_GROUPS)]
        o_ref[s, rows, :] = jnp.concatenate(outs, axis=1)

    def chunk(c, carry):
        r0 = pl.multiple_of(c * CH, CH)
        for s in range(NBP):
            one(s, r0, pl.ds(r0, CH))
        return carry

    lax.fori_loop(0, nch, chunk, 0)


def _ssd_call(proj3, tabs, cwx, cwb, cbx, cbb, dtb, a, dsk, nw, l, tt):
    tp = proj3.shape[1]
    ltri, expand, hm = tabs
    return pl.pallas_call(
        functools.partial(_ssd_kernel, nch=tt // CH, tt=tt),
        grid=(NB // NBP, tp // tt),
        in_specs=[_mixer_grp_spec(tt, G_Z), _mixer_grp_spec(tt, G_XS), _mixer_grp_spec(tt, G_BC),
                  _mixer_grp_spec(tt, G_DT),
                  _const_spec(ltri.shape), _const_spec(expand.shape), _const_spec(hm.shape),
                  _layer_spec((SSD_CONV, GW), l), _layer_spec((SSD_CONV, GW), l),
                  _layer_spec((1, GW), l), _layer_spec((1, GW), l), _layer_spec((1, 128), l),
                  _layer_spec((1, 128), l), _layer_spec((1, GW), l), _layer_spec((1, GW), l)],
        out_specs=pl.BlockSpec((NBP, tt, GW), lambda bb, i: (bb, i, 0)),
        out_shape=jax.ShapeDtypeStruct((NB, tp, GW), F32),
        scratch_shapes=[pltpu.VMEM((NBP, SSD_GROUPS, SSD_STATE, 256), F32),
                        pltpu.VMEM((NBP, CH + 2 * NB, 2 * GW), F32)],
        compiler_params=_mixer_params(2 * 5 * NBP * tt * GW * 4 + (4 << 20)),
        name="ssd",
    )(proj3, proj3, proj3, proj3, ltri, expand, hm, cwx, cwb, cbx, cbb, dtb, a, dsk, nw)


HG_LEVELS = (64, 32, 16, 8, 4, 2)


def _hg_tables():
    t = np.arange(CH)
    blocks = []
    for c in HG_LEVELS:
        m = np.zeros((CH, CH), np.float32)
        for r in range(CH):
            start = (r // c) * c
            mid = start + c // 2 - 1
            if r <= mid:
                m[r, r + 1:mid + 1] = 1.0
            else:
                m[r, mid + 1:r + 1] = 1.0
        blocks.append(m)
    blocks.append((t[:, None] >= t[None, :]).astype(np.float32))
    blocks.append((t[None, :] > t[:, None]).astype(np.float32))
    tall = np.concatenate(blocks, axis=0)
    ones_blk = np.kron(np.eye(HG_HEADS, dtype=np.float32), np.ones((HG_DK, HG_DK), np.float32))
    pmask = np.kron(np.eye(2, dtype=np.float32), np.ones((128, 128), np.float32))
    return jnp.asarray(tall, BF16), jnp.asarray(ones_blk, BF16), jnp.asarray(pmask, F32)


def _hg_kernel(q_ref, f_ref, i_ref, g_ref, tall_ref, ones_ref, pmask_ref, lb_ref, nw_ref, o_ref,
               st_ref, *, nch):
    @pl.when(pl.program_id(1) == 0)
    def _():
        st_ref[...] = jnp.zeros_like(st_ref)

    row = lax.broadcasted_iota(jnp.int32, (CH, 1), 0)
    trow = lax.broadcasted_iota(jnp.int32, (CH, 128), 0)
    scol = lax.broadcasted_iota(jnp.int32, (CH, 128), 1) % CH
    zero128 = jnp.zeros((CH, 128), BF16)

    def one(s, rows):
        lb = lb_ref[...]
        f = lb + (1.0 - lb) * _sigmoid(f_ref[s, rows, :])
        logf = jnp.log(f)
        k = 1.0 - f
        q = _silu(q_ref[s, rows, :])
        v = i_ref[s, rows, :]
        vb = v.astype(BF16)
        zall = _sel_dot(tall_ref[...], logf)
        nl = len(HG_LEVELS)
        cum = zall[nl * CH:(nl + 1) * CH, :]
        rem = zall[(nl + 1) * CH:(nl + 2) * CH, :]
        acc = [jnp.zeros((CH, 128), F32), jnp.zeros((CH, 128), F32)]
        for li, cs in enumerate(HG_LEVELS):
            fh = (row % cs) < (cs // 2)
            x = jnp.where(fh, k, q) * jnp.exp(zall[li * CH:(li + 1) * CH, :])
            qt = jnp.where(fh, 0.0, x).astype(BF16)
            kt = jnp.where(fh, x, 0.0).astype(BF16)
            lmask = (((trow // cs) == (scol // cs)) & ((trow % cs) >= (cs // 2))
                     & ((scol % cs) < (cs // 2))).astype(F32)
            for p in range(2):
                a0 = kt[:, (2 * p) * 128:(2 * p + 1) * 128]
                a1 = kt[:, (2 * p + 1) * 128:(2 * p + 2) * 128]
                kstack = jnp.concatenate([jnp.concatenate([a0, zero128], axis=1),
                                          jnp.concatenate([zero128, a1], axis=1)], axis=0)
                sc = _dot_nt(qt[:, p * 256:(p + 1) * 256], kstack)
                acc[p] = acc[p] + sc * lmask
        r_same = _dot((q * k).astype(BF16), ones_ref[...])
        qd = (q * jnp.exp(cum)).astype(BF16)
        kw = (k * jnp.exp(rem)).astype(BF16)
        elast = jnp.exp(cum[CH - 1:CH, :])
        ys = []
        for p in range(2):
            pl_ = slice(p * 256, (p + 1) * 256)
            v0 = vb[:, (2 * p) * 128:(2 * p + 1) * 128]
            v1 = vb[:, (2 * p + 1) * 128:(2 * p + 2) * 128]
            vstack = jnp.concatenate([jnp.concatenate([v0, zero128], axis=1),
                                      jnp.concatenate([zero128, v1], axis=1)], axis=0)
            st = st_ref[s, p]
            yp = _dot(acc[p].astype(BF16), vstack) + _dot_nt(qd[:, pl_], st.astype(BF16))
            st_ref[s, p] = st * elast[:, pl_] + _dot_tn(vb[:, pl_], kw[:, pl_]) * pmask_ref[...]
            ys.append(yp)
        o = jnp.concatenate(ys, axis=1) + r_same * v
        outs = [_rms(o[:, h * 128:(h + 1) * 128], nw_ref[:, h * 128:(h + 1) * 128])
                for h in range(HG_HEADS)]
        o_ref[s, rows, :] = jnp.concatenate(outs, axis=1) * _silu(g_ref[s, rows, :])

    def chunk(c, carry):
        rows = pl.ds(pl.multiple_of(c * CH, CH), CH)
        for s in range(NBP):
            one(s, rows)
        return carry

    lax.fori_loop(0, nch, chunk, 0)


def _hg_call(proj3, tabs, lb, nw, l, tt):
    tp = proj3.shape[1]
    tall, ones_blk, pmask = tabs
    return pl.pallas_call(
        functools.partial(_hg_kernel, nch=tt // CH),
        grid=(NB // NBP, tp // tt),
        in_specs=[_mixer_grp_spec(tt, G_HQ), _mixer_grp_spec(tt, G_HF), _mixer_grp_spec(tt, G_HI),
                  _mixer_grp_spec(tt, G_HG),
                  _const_spec(tall.shape), _const_spec(ones_blk.shape), _const_spec(pmask.shape),
                  _layer_spec((1, GW), l), _layer_spec((1, GW), l)],
        out_specs=pl.BlockSpec((NBP, tt, GW), lambda bb, i: (bb, i, 0)),
        out_shape=jax.ShapeDtypeStruct((NB, tp, GW), F32),
        scratch_shapes=[pltpu.VMEM((NBP, 2, 256, 256), F32)],
        compiler_params=_mixer_params(2 * 5 * NBP * tt * GW * 4 + (4 << 20)),
        name="hgrn2",
    )(proj3, proj3, proj3, proj3, tall, ones_blk, pmask, lb, nw)


def _merge_kernel(h_ref, y0_ref, y1_ref, y2_ref, y3_ref, npre_ref, wg_ref, wb_ref, wo_ref,
                  npost_ref, o_ref, *, tm, tps):
    h = h_ref[...]
    u = _rms(h, npre_ref[...]).astype(BF16)
    mixed = jnp.zeros((tm, D_MODEL), F32)
    for n, y_ref in enumerate((y0_ref, y1_ref, y2_ref, y3_ref)):
        gate = _sigmoid(_dot(u, wg_ref[:, n * D_MODEL:(n + 1) * D_MODEL]))
        mixed = mixed + gate * _dot(y_ref[...].astype(BF16), wb_ref[n])
    out = _dot(mixed.astype(BF16), wo_ref[...])
    valid = _valid_rows(pl.program_id(0), tps, tm)
    o_ref[...] = jnp.where(valid, h + _rms(out, npost_ref[...]), 0.0)


def _merge_call(h, ys, npre, wg, wb, wo, npost, l, tm, tps):
    rows = h.shape[0]
    nbytes = (2 * (2 * tm * D_MODEL * 4 + 4 * tm * GW * 4)
              + (D_MODEL * 4 * D_MODEL + 4 * GW * D_MODEL + D_MODEL * D_MODEL) * 2
              + 6 * tm * D_MODEL * 4)
    yspec = pl.BlockSpec((tm, GW), lambda i: (i, 0))
    return pl.pallas_call(
        functools.partial(_merge_kernel, tm=tm, tps=tps),
        grid=(rows // tm,),
        in_specs=[pl.BlockSpec((tm, D_MODEL), lambda i: (i, 0)), yspec, yspec, yspec, yspec,
                  _layer_spec((1, D_MODEL), l), _layer_spec((D_MODEL, 4 * D_MODEL), l),
                  _layer_spec((N_BRANCH, GW, D_MODEL), l), _layer_spec((D_MODEL, D_MODEL), l),
                  _layer_spec((1, D_MODEL), l)],
        out_specs=pl.BlockSpec((tm, D_MODEL), lambda i: (i, 0)),
        out_shape=jax.ShapeDtypeStruct((rows, D_MODEL), F32),
        compiler_params=pltpu.CompilerParams(
            dimension_semantics=("parallel",), vmem_limit_bytes=_vmem_limit(nbytes)),
        name="merge",
    )(h, *ys, npre, wg, wb, wo, npost)


def _mlp_kernel(h_ref, npre_ref, wu_ref, wd_ref, npost_ref, o_ref, *, tm, tps, fc):
    h = h_ref[...]
    m = _rms(h, npre_ref[...]).astype(BF16)
    acc = jnp.zeros((tm, D_MODEL), F32)
    for c in range(D_FF // fc):
        a = jnp.maximum(_dot(m, wu_ref[:, c * fc:(c + 1) * fc]), 0.0)
        acc = acc + _dot((a * a).astype(BF16), wd_ref[c * fc:(c + 1) * fc, :])
    valid = _valid_rows(pl.program_id(0), tps, tm)
    o_ref[...] = jnp.where(valid, h + _rms(acc, npost_ref[...]), 0.0)


def _mlp_call(h, npre, wu, wd, npost, l, tm, tps):
    rows = h.shape[0]
    nbytes = 2 * (2 * tm * D_MODEL * 4) + (2 * D_MODEL * D_FF * 2) + 6 * tm * D_MODEL * 4
    return pl.pallas_call(
        functools.partial(_mlp_kernel, tm=tm, tps=tps, fc=1024),
        grid=(rows // tm,),
        in_specs=[pl.BlockSpec((tm, D_MODEL), lambda i: (i, 0)), _layer_spec((1, D_MODEL), l),
                  _layer_spec((D_MODEL, D_FF), l), _layer_spec((D_FF, D_MODEL), l),
                  _layer_spec((1, D_MODEL), l)],
        out_specs=pl.BlockSpec((tm, D_MODEL), lambda i: (i, 0)),
        out_shape=jax.ShapeDtypeStruct((rows, D_MODEL), F32),
        compiler_params=pltpu.CompilerParams(
            dimension_semantics=("parallel",), vmem_limit_bytes=_vmem_limit(nbytes)),
        name="mlp",
    )(h, npre, wu, wd, npost)


def _prep_w_in(w_in):
    depth = w_in.shape[0]
    perm = np.zeros(256, np.int64)
    for half in range(2):
        for h in range(RET_HEADS):
            for i in range(32):
                perm[half * 128 + h * 32 + i] = h * 64 + half * 32 + i
    rq = w_in[:, :, 0:256][:, :, perm]
    rk = w_in[:, :, 256:512][:, :, perm]
    rest = w_in[:, :, 512:3584]
    hg = w_in[:, :, 3592:5640]
    dt = jnp.pad(w_in[:, :, 3584:3592], ((0, 0), (0, 0), (0, GW - SSD_HEADS)))
    wa = jnp.concatenate([rq, rk, rest, hg, dt], axis=2).astype(BF16)
    wg = w_in[:, :, 5640:].astype(BF16)
    assert wa.shape == (depth, D_MODEL, NA)
    return wa, wg


def _prep_s5(lam_re, lam_im, b_re, b_im, c_re, c_im, log_step):
    depth = lam_re.shape[0]
    step = jnp.exp(log_step.astype(F32))[..., None]
    lr, li = lam_re.astype(F32), lam_im.astype(F32)
    mag = jnp.exp(lr * step)
    ab_re, ab_im = mag * jnp.cos(li * step), mag * jnp.sin(li * step)
    inv = 1.0 / (lr * lr + li * li)
    co_re = ((ab_re - 1.0) * lr + ab_im * li) * inv
    co_im = (ab_im * lr - (ab_re - 1.0) * li) * inv
    br, bi = b_re.astype(F32), b_im.astype(F32)
    bb_re = co_re[..., None] * br - co_im[..., None] * bi
    bb_im = co_re[..., None] * bi + co_im[..., None] * br
    eye = jnp.eye(S5_GROUPS, dtype=F32)
    gps = S5_GROUPS // S5_SG
    eye5 = eye[None, :, None, :, None]
    bd = jnp.stack([m.transpose(0, 1, 3, 2)[:, :, :, None, :] * eye5 for m in (bb_re, bb_im)], axis=3)
    bd = bd.reshape(depth, GW, 2, S5_SG, gps, S5_STATE).transpose(0, 1, 3, 2, 4, 5)
    bd = bd.reshape(depth, GW, 2 * S5_P2)
    n_tiles = 2 * S5_P2 // MXU_T
    bm = bd.reshape(depth, GW // MXU_T, MXU_T, n_tiles, MXU_T).sum(axis=1)
    bm = bm.transpose(0, 2, 1, 3).astype(BF16)
    c_parts = (c_re.astype(F32), c_im.astype(F32) * (-1.0))
    cd = jnp.stack([m.transpose(0, 1, 3, 2)[:, :, :, None, :] * eye5
                    for m in c_parts], axis=1)
    cd = cd.reshape(depth, 2, S5_SG, gps, S5_STATE, GW).transpose(0, 2, 1, 3, 4, 5)
    cd = cd.reshape(depth, S5_SG, S5_SGW, GW)
    cm = cd.reshape(depth, S5_SG, S5_SGW, GW // MXU_T, MXU_T).sum(axis=3).astype(BF16)
    ar = jnp.broadcast_to(ab_re.reshape(depth, 1, S5_P2), (depth, NB, S5_P2))
    ai = jnp.broadcast_to(ab_im.reshape(depth, 1, S5_P2), (depth, NB, S5_P2))
    return bm, ar, ai, cm


def _rope_tables(tp):
    half = RET_DK // 2
    inv_freq = RET_ROPE_BASE ** (-jnp.arange(half, dtype=F32) / half)
    pos = jnp.maximum(jnp.arange(tp, dtype=F32) - PADF, 0.0)
    ang = pos[:, None] * inv_freq[None, :]
    return jnp.tile(jnp.cos(ang), (1, RET_HEADS)), jnp.tile(jnp.sin(ang), (1, RET_HEADS))


def _pick_tiles(tp):
    tt = 192 if tp % 192 == 0 else CH
    tm = next(c for c in (704, 192, CH) if tp % c == 0)
    tm_in = next(c for c in (1056, 192, CH) if tp % c == 0)
    return tt, tm, tm_in


def kernel(x, meta_tokens, w_in, w_branch, w_out, norm_pre_mix, norm_post_mix, norm_pre_mlp, norm_post_mlp, w_up, w_down, ret_gn_w, s5_lam_re, s5_lam_im, s5_b_re, s5_b_im, s5_c_re, s5_c_im, s5_d, s5_log_step, s5_glu_w, s5_glu_b, ssd_conv_w, ssd_conv_b, ssd_dt_bias, ssd_a_log, ssd_d, ssd_norm_w, hgrn_lb, hgrn_norm_w):
    bsz, seq, _ = x.shape
    depth = w_in.shape[0]
    assert bsz == NB and x.shape[2] == D_MODEL
    tp = PADF + N_META + seq
    assert tp % CH == 0
    tt, tm, tm_in = _pick_tiles(tp)
    rows = tp * NB

    meta = jnp.broadcast_to(meta_tokens[None].astype(x.dtype), (bsz, N_META, D_MODEL))
    h = jnp.concatenate([jnp.zeros((bsz, PADF, D_MODEL), x.dtype), meta, x], axis=1)
    h = h.reshape(rows, D_MODEL).astype(F32)

    wa, wg = _prep_w_in(w_in)
    wb = w_branch.astype(BF16)
    wo = w_out.astype(BF16)
    wu = w_up.astype(BF16)
    wd = w_down.astype(BF16)
    r3 = lambda a: a.astype(F32).reshape(depth, 1, -1)
    npre, npost, mpre, mpost = r3(norm_pre_mix), r3(norm_post_mix), r3(norm_pre_mlp), r3(norm_post_mlp)
    bm, ar, ai, cm = _prep_s5(s5_lam_re, s5_lam_im, s5_b_re, s5_b_im, s5_c_re, s5_c_im, s5_log_step)
    s5gw = s5_glu_w.astype(BF16)
    s5gb = r3(s5_glu_b)
    s5d = r3(s5_d)
    cwx = ssd_conv_w[:, :, :GW].astype(F32)
    cwb = ssd_conv_w[:, :, GW:].astype(F32)
    cbx = r3(ssd_conv_b[:, :GW])
    cbb = r3(ssd_conv_b[:, GW:])
    pad8 = lambda a: jnp.pad(a.astype(F32), ((0, 0), (0, 128 - SSD_HEADS))).reshape(depth, 1, 128)
    dtb = pad8(ssd_dt_bias)
    ssd_a = pad8(-jnp.exp(ssd_a_log.astype(F32)))
    ssd_dsk = r3(jnp.repeat(ssd_d.astype(F32), SSD_HEADDIM, axis=1))
    ssd_nw = r3(ssd_norm_w)
    lb_all = jnp.cumsum(jax.nn.softmax(hgrn_lb.astype(F32), axis=0), axis=0)
    lb_all = r3(lb_all - lb_all[0])
    hg_nw = r3(hgrn_norm_w)
    gnw = r3(ret_gn_w)

    cos_t, sin_t = _rope_tables(tp)
    ret_tabs, ssd_tabs, hg_tabs = _ret_tables(), _ssd_tables(), _hg_tables()

    flat = lambda y: y.reshape(rows, GW)
    for l in range(depth):
        proj3 = _inproj(h, npre, wa, l, tm_in, 1536).reshape(NB, tp, NA)
        y_ret = _ret_call(proj3, cos_t, sin_t, ret_tabs, gnw, l, tt)
        y_s5 = _s5_call(proj3, bm, ar, ai, cm, s5d, s5gw, s5gb, l, CH)
        y_ssd = _ssd_call(proj3, ssd_tabs, cwx, cwb, cbx, cbb, dtb, ssd_a, ssd_dsk, ssd_nw, l, tt)
        y_hg = _hg_call(proj3, hg_tabs, lb_all, hg_nw, l, tt)
        h = _merge_call(h, (flat(y_ret), flat(y_s5), flat(y_ssd), flat(y_hg)),
                        npre, wg, wb, wo, npost, l, tm, tp // tm)
        h = _mlp_call(h, mpre, wu, wd, mpost, l, tm, tp // tm)

    return h.reshape(NB, tp, D_MODEL)[:, PADF + N_META:].astype(x.dtype)
```

```python
import functools

import numpy as np
import jax
import jax.numpy as jnp
from jax import lax
from jax.experimental import pallas as pl
from jax.experimental.pallas import tpu as pltpu

F32 = jnp.float32
BF16 = jnp.bfloat16

D_MODEL = 1024
N_META = 16
BRANCH_WIDTH = 512
N_BRANCH = 4
D_FF = 4 * D_MODEL
NORM_EPS = 1e-6
RET_HEADS = 4
RET_DK = 64
RET_DV = 128
RET_ROPE_BASE = 10000.0
S5_GROUP = 16
S5_GROUPS = 32
S5_STATE = 64
SSD_HEADDIM = 64
SSD_HEADS = 8
SSD_GROUPS = 2
SSD_STATE = 128
SSD_CONV = 4
HG_HEADS = 4
HG_DK = 128

NB = 8
PADF = 48
CH = 64
GW = 512
N_GROUPS = 12
NA = GW * N_GROUPS
G_QK, G_RV, G_RG, G_S5, G_Z, G_XS, G_BC, G_HQ, G_HF, G_HI, G_HG, G_DT = range(N_GROUPS)
S5_P2 = S5_GROUPS * S5_STATE
S5_SG = 4
S5_SGW = 2 * S5_P2 // S5_SG
MXU_T = 256
NBP = 4
VMEM_LIMIT_CAP = 56 * 1024 * 1024


def _vmem_limit(nbytes):
    return int(min(VMEM_LIMIT_CAP, max(32 * 1024 * 1024, nbytes * 5 // 4 + 8 * 1024 * 1024)))


def _const_spec(shape):
    nd = len(shape)
    return pl.BlockSpec(shape, lambda *a: (0,) * nd, pipeline_mode=pl.Buffered(1))


def _layer_spec(shape, l):
    nd = len(shape)
    return pl.BlockSpec((None,) + tuple(shape), lambda *a: (l,) + (0,) * nd,
                        pipeline_mode=pl.Buffered(1))


def _split_bf16(x):
    hi = x.astype(BF16)
    lo = (x - hi.astype(F32)).astype(BF16)
    return hi, lo


def _dot(a, b):
    return jnp.dot(a, b, preferred_element_type=F32)


def _dot_nt(a, b):
    return lax.dot_general(a, b, (((1,), (1,)), ((), ())), preferred_element_type=F32)


def _dot_tn(a, b):
    return lax.dot_general(a, b, (((0,), (0,)), ((), ())), preferred_element_type=F32)


def _dot_sel(x, e_bf16):
    hi, lo = _split_bf16(x)
    return _dot(hi, e_bf16) + _dot(lo, e_bf16)


def _sigmoid(x):
    return 1.0 / (1.0 + jnp.exp(-x))


def _silu(x):
    return x * _sigmoid(x)


def _rms(x, w):
    ms = jnp.mean(x * x, axis=-1, keepdims=True)
    return x * lax.rsqrt(ms + NORM_EPS) * w


def _valid_rows(tile_idx, tiles_per_seq, tm):
    tpos = (tile_idx % tiles_per_seq) * tm + lax.broadcasted_iota(jnp.int32, (tm, 1), 0)
    return tpos >= PADF


def _inproj_kernel(h_ref, nw_ref, w_ref, cos_ref, sin_ref, cw_ref, cb_ref, lb_ref, dtb_ref, a_ref,
                   o_ref, lfl_ref, hk_ref, xn_ref, *, tm):
    j = pl.program_id(1)

    @pl.when(j == 0)
    def _():
        xn_ref[...] = _rms(h_ref[...], nw_ref[...]).astype(BF16)

    def mm():
        return _dot(xn_ref[...], w_ref[...])

    @pl.when((j == G_RV) | (j == G_S5) | (j == G_HI))
    def _():
        o_ref[...] = mm().astype(BF16)

    @pl.when((j == G_RG) | (j == G_Z) | (j == G_HQ) | (j == G_HG))
    def _():
        o_ref[...] = _silu(mm()).astype(BF16)

    @pl.when((j == G_XS) | (j == G_BC))
    def _():
        r = mm()
        acc = cb_ref[...] + r * cw_ref[SSD_CONV - 1:SSD_CONV, :]
        for back in range(1, SSD_CONV):
            tap = SSD_CONV - 1 - back
            acc = acc + pltpu.roll(r, back, 0) * cw_ref[tap:tap + 1, :]
        valid = _valid_rows(0, 1, tm)
        o_ref[...] = jnp.where(valid, _silu(acc), 0.0).astype(BF16)

    @pl.when(j == G_QK)
    def _():
        r = mm()
        cs = cos_ref[...]
        sn = sin_ref[...]
        q1, q2, k1, k2 = r[:, 0:128], r[:, 128:256], r[:, 256:384], r[:, 384:512]
        o_ref[:, 0:128] = (q1 * cs - q2 * sn).astype(BF16)
        o_ref[:, 128:256] = (q1 * sn + q2 * cs).astype(BF16)
        o_ref[:, 256:384] = ((k1 * cs - k2 * sn) * (RET_DK ** -0.5)).astype(BF16)
        o_ref[:, 384:512] = ((k1 * sn + k2 * cs) * (RET_DK ** -0.5)).astype(BF16)

    @pl.when(j == G_HF)
    def _():
        lb = lb_ref[...]
        f = lb + (1.0 - lb) * _sigmoid(mm())
        hi, lo = _split_bf16(jnp.log(f))
        o_ref[...] = hi
        lfl_ref[...] = lo
        hk_ref[...] = (1.0 - f).astype(BF16)

    @pl.when(j == G_DT)
    def _():
        dt = jax.nn.softplus(mm()[:, 0:128] + dtb_ref[...])
        dhi, dlo = _split_bf16(dt)
        lhi, llo = _split_bf16(dt * a_ref[...])
        o_ref[:, 0:128] = dhi
        o_ref[:, 128:256] = dlo
        o_ref[:, 256:384] = lhi
        o_ref[:, 384:512] = llo


def _inproj(h, nw, w, cos_t, sin_t, cw, cb, lb, dtb, a, l, tm):
    rows = h.shape[0]
    assert cos_t.shape[0] == tm
    nbytes = (tm * D_MODEL * 4 + tm * D_MODEL * 2 + 2 * D_MODEL * GW * 2 + 3 * 2 * tm * GW * 2
              + 2 * tm * 128 * 4 + 5 * tm * GW * 4)
    ospec0 = pl.BlockSpec((tm, GW), lambda i, j: (i, 0))
    return pl.pallas_call(
        functools.partial(_inproj_kernel, tm=tm),
        grid=(rows // tm, N_GROUPS),
        in_specs=[
            pl.BlockSpec((tm, D_MODEL), lambda i, j: (i, 0), pipeline_mode=pl.Buffered(1)),
            pl.BlockSpec((None, 1, D_MODEL), lambda i, j: (l, 0, 0)),
            pl.BlockSpec((None, D_MODEL, GW), lambda i, j: (l, 0, j)),
            _const_spec((tm, 128)), _const_spec((tm, 128)),
            pl.BlockSpec((None, SSD_CONV, GW), lambda i, j: (l, 0, j)),
            pl.BlockSpec((None, 1, GW), lambda i, j: (l, 0, j)),
            _layer_spec((1, GW), l), _layer_spec((1, 128), l), _layer_spec((1, 128), l),
        ],
        out_specs=[pl.BlockSpec((tm, GW), lambda i, j: (i, j)), ospec0, ospec0],
        out_shape=[jax.ShapeDtypeStruct((rows, NA), BF16), jax.ShapeDtypeStruct((rows, GW), BF16),
                   jax.ShapeDtypeStruct((rows, GW), BF16)],
        scratch_shapes=[pltpu.VMEM((tm, D_MODEL), BF16)],
        compiler_params=pltpu.CompilerParams(
            dimension_semantics=("parallel", "arbitrary"), vmem_limit_bytes=_vmem_limit(nbytes)),
        name="inproj",
    )(h, nw, w, cos_t, sin_t, cw, cb, lb, dtb, a)


def _mixer_grp_spec(tt, g):
    return pl.BlockSpec((NBP, tt, GW), lambda bb, i, g=g: (bb, i, g))


def _mixer_params(est_bytes):
    return pltpu.CompilerParams(dimension_semantics=("parallel", "arbitrary"),
                                vmem_limit_bytes=_vmem_limit(est_bytes))


def _ret_tables():
    gam = 1.0 - np.exp2(-5.0 - np.arange(RET_HEADS, dtype=np.float64))
    lg = np.log(gam)
    t = np.arange(CH)
    col_head = (np.arange(2 * 128) % 128) // 32
    dall = np.zeros((CH, RET_HEADS * CH), np.float64)
    for h in range(RET_HEADS):
        d = np.exp(lg[h] * (t[:, None] - t[None, :]))
        dall[:, h * CH:(h + 1) * CH] = np.where(t[:, None] >= t[None, :], d, 0.0)
    qdec = np.exp(lg[col_head][None, :] * (t[:, None] + 1.0))
    kdec = np.exp(lg[col_head][None, :] * (CH - 1.0 - t[:, None]))
    vcol_head = np.arange(RET_HEADS * RET_DV) // RET_DV
    sdec = np.exp(lg[col_head] * CH)[:, None] * np.ones((1, 512))
    smask = (col_head[:, None] == vcol_head[None, :]).astype(np.float64)
    kmask = (col_head[None, :] == np.arange(RET_HEADS)[:, None]).astype(np.float64)[:, None, :]
    vmask = (vcol_head[None, :] == np.arange(RET_HEADS)[:, None]).astype(np.float64)[:, None, :]
    f = lambda a: jnp.asarray(a, F32)
    b = lambda a: jnp.asarray(a, BF16)
    return f(dall), f(qdec), f(kdec), f(sdec), f(smask), b(kmask), b(vmask)


def _ret_kernel(qk_ref, v_ref, g_ref, dall_ref, qdec_ref, kdec_ref, sdec_ref,
                smask_ref, kmask_ref, vmask_ref, gnw_ref, o_ref, st_ref, *, nch):
    @pl.when(pl.program_id(1) == 0)
    def _():
        st_ref[...] = jnp.zeros_like(st_ref)

    def one(s, rows):
        qk = qk_ref[s, rows, :]
        qb = qk[:, 0:256]
        kb = qk[:, 256:512]
        vb = v_ref[s, rows, :]
        kstack = jnp.concatenate([kb * kmask_ref[h] for h in range(RET_HEADS)], axis=0)
        s_all = _dot_nt(qb, kstack)
        p = (s_all * dall_ref[...]).astype(BF16)
        vstack = jnp.concatenate([vb * vmask_ref[h] for h in range(RET_HEADS)], axis=0)
        y = _dot(p, vstack)
        st = st_ref[s]
        y = y + _dot((qb.astype(F32) * qdec_ref[...]).astype(BF16), st.astype(BF16))
        new = _dot_tn((kb.astype(F32) * kdec_ref[...]).astype(BF16), vb)
        st_ref[s] = st * sdec_ref[...] + new * smask_ref[...]
        outs = []
        for h in range(RET_HEADS):
            yh = y[:, h * RET_DV:(h + 1) * RET_DV]
            mu = jnp.mean(yh, axis=-1, keepdims=True)
            d = yh - mu
            var = jnp.mean(d * d, axis=-1, keepdims=True)
            outs.append(d * lax.rsqrt(var + NORM_EPS))
        yn = jnp.concatenate(outs, axis=1) * gnw_ref[...]
        o_ref[s, rows, :] = g_ref[s, rows, :].astype(F32) * yn

    def chunk(c, carry):
        rows = pl.ds(pl.multiple_of(c * CH, CH), CH)
        for s in range(NBP):
            one(s, rows)
        return carry

    lax.fori_loop(0, nch, chunk, 0)


def _ret_call(proj3, tabs, gnw, l, tt):
    tp = proj3.shape[1]
    dall, qdec, kdec, sdec, smask, kmask, vmask = tabs
    return pl.pallas_call(
        functools.partial(_ret_kernel, nch=tt // CH),
        grid=(NB // NBP, tp // tt),
        in_specs=[_mixer_grp_spec(tt, G_QK), _mixer_grp_spec(tt, G_RV), _mixer_grp_spec(tt, G_RG),
                  _const_spec(dall.shape), _const_spec(qdec.shape), _const_spec(kdec.shape),
                  _const_spec(sdec.shape), _const_spec(smask.shape), _const_spec(kmask.shape),
                  _const_spec(vmask.shape), _layer_spec((1, GW), l)],
        out_specs=pl.BlockSpec((NBP, tt, GW), lambda bb, i: (bb, i, 0)),
        out_shape=jax.ShapeDtypeStruct((NB, tp, GW), F32),
        scratch_shapes=[pltpu.VMEM((NBP, 256, 512), F32)],
        compiler_params=_mixer_params(2 * NBP * tt * GW * (3 * 2 + 4) + (4 << 20)),
        name="retention",
    )(proj3, proj3, proj3, dall, qdec, kdec, sdec, smask, kmask, vmask, gnw)


def _s5_kernel(u_ref, bm_ref, ar_ref, ai_ref, cm_ref, d_ref, gw_ref, gb_ref, o_ref,
               utm_ref, ytm_ref, bu_ref, xs_ref, *, tt):
    @pl.when(pl.program_id(0) == 0)
    def _():
        xs_ref[...] = jnp.zeros_like(xs_ref)

    for b in range(NB):
        utm_ref[:, b, :] = u_ref[b].astype(F32)
    u = utm_ref[...].reshape(tt * NB, GW)
    ub = u.astype(BF16)
    half = S5_SGW // 2
    for nt in range(2 * S5_P2 // MXU_T):
        kb = nt // (2 * S5_P2 // MXU_T // (GW // MXU_T))
        bu_ref[:, nt * MXU_T:(nt + 1) * MXU_T] = _dot(ub[:, kb * MXU_T:(kb + 1) * MXU_T], bm_ref[nt])
    for c in range(S5_SG):
        cr = pl.ds(c * S5_SGW, half)
        ci = pl.ds(c * S5_SGW + half, half)
        ar = ar_ref[:, pl.ds(c * half, half)]
        ai = ai_ref[:, pl.ds(c * half, half)]

        def step(t, carry):
            xr, xi = carry
            rows = pl.ds(pl.multiple_of(t * NB, NB), NB)
            nr = ar * xr - ai * xi + bu_ref[rows, cr]
            ni = ar * xi + ai * xr + bu_ref[rows, ci]
            bu_ref[rows, cr] = nr
            bu_ref[rows, ci] = ni
            return nr, ni

        xr, xi = lax.fori_loop(0, tt, step, (xs_ref[:, cr], xs_ref[:, ci]), unroll=4)
        xs_ref[:, cr] = xr
        xs_ref[:, ci] = xi
    ys = []
    for j in range(GW // MXU_T):
        acc = None
        for c in range(j * (S5_SG // 2), (j + 1) * (S5_SG // 2)):
            part = _dot(bu_ref[:, c * S5_SGW:(c + 1) * S5_SGW].astype(BF16), cm_ref[c])
            acc = part if acc is None else acc + part
        ys.append(acc)
    y = jnp.concatenate(ys, axis=1) + d_ref[...] * u
    y = jax.nn.gelu(y)
    ag = _dot(y.astype(BF16), gw_ref[...]) + gb_ref[...]
    ytm_ref[...] = (ag[:, :BRANCH_WIDTH] * _sigmoid(ag[:, BRANCH_WIDTH:])).reshape(tt, NB, GW)
    for b in range(NB):
        o_ref[b] = ytm_ref[:, b, :]


def _s5_call(proj3, bm, ar, ai, cm, dsk, gw, gb, l, tt):
    tp = proj3.shape[1]
    rb = tt * NB
    nbytes = (2 * rb * GW * 2 + 2 * rb * GW * 4 + 2 * rb * GW * 4 + rb * 2 * S5_P2 * 4 * 2
              + (bm.shape[1] * MXU_T * MXU_T + cm.shape[1] * S5_SGW * MXU_T + GW * 2 * GW) * 2)
    return pl.pallas_call(
        functools.partial(_s5_kernel, tt=tt),
        grid=(tp // tt,),
        in_specs=[pl.BlockSpec((NB, tt, GW), lambda i: (0, i, G_S5)),
                  _layer_spec(bm.shape[1:], l), _layer_spec((NB, S5_P2), l),
                  _layer_spec((NB, S5_P2), l), _layer_spec(cm.shape[1:], l),
                  _layer_spec((1, GW), l), _layer_spec((GW, 2 * GW), l), _layer_spec((1, 2 * GW), l)],
        out_specs=pl.BlockSpec((NB, tt, GW), lambda i: (0, i, 0)),
        out_shape=jax.ShapeDtypeStruct((NB, tp, GW), F32),
        scratch_shapes=[pltpu.VMEM((tt, NB, GW), F32), pltpu.VMEM((tt, NB, GW), F32),
                        pltpu.VMEM((rb, 2 * S5_P2), F32), pltpu.VMEM((NB, 2 * S5_P2), F32)],
        compiler_params=pltpu.CompilerParams(
            dimension_semantics=("arbitrary",), vmem_limit_bytes=_vmem_limit(nbytes)),
        name="s5",
    )(proj3, bm, ar, ai, cm, dsk, gw, gb)


def _ssd_tables():
    t = np.arange(CH)
    ltri = (t[:, None] >= t[None, :]).astype(np.float32)
    expand = np.zeros((128, GW), np.float32)
    for h in range(SSD_HEADS):
        expand[h, h * SSD_HEADDIM:(h + 1) * SSD_HEADDIM] = 1.0
    hm = np.zeros((4, 1, 256), np.float32)
    for h in range(4):
        hm[h, 0, h * SSD_HEADDIM:(h + 1) * SSD_HEADDIM] = 1.0
    return jnp.asarray(ltri, BF16), jnp.asarray(expand, BF16), jnp.asarray(hm, BF16)


def _ssd_kernel(z_ref, xs_ref, bc_ref, dt_ref, ltri_ref, exp_ref, hm_ref, dsk_ref, nw_ref, o_ref,
                st_ref, *, nch):
    @pl.when(pl.program_id(1) == 0)
    def _():
        st_ref[...] = jnp.zeros_like(st_ref)

    tri = lax.broadcasted_iota(jnp.int32, (CH, CH), 0) >= lax.broadcasted_iota(jnp.int32, (CH, CH), 1)
    hpg = SSD_HEADS // SSD_GROUPS

    def one(s, rows):
        xs = xs_ref[s, rows, :].astype(F32)
        bc = bc_ref[s, rows, :]
        dtg = dt_ref[s, rows, :]
        cum = _dot(ltri_ref[...], dtg[:, 256:384]) + _dot(ltri_ref[...], dtg[:, 384:512])
        dt_e = _dot(dtg[:, 0:128], exp_ref[...]) + _dot(dtg[:, 128:256], exp_ref[...])
        cum_e = _dot_sel(cum, exp_ref[...])
        last_e = cum_e[CH - 1:CH, :]
        v = xs * dt_e
        vb = v.astype(BF16)
        wv = (v * jnp.exp(last_e - cum_e)).astype(BF16)
        ecum = jnp.exp(cum_e)
        elast = jnp.exp(last_e)
        cum_t = cum.T
        ys = []
        for g in range(SSD_GROUPS):
            gl = slice(g * 256, (g + 1) * 256)
            bm = bc[:, g * 128:(g + 1) * 128]
            cm = bc[:, 256 + g * 128:256 + (g + 1) * 128]
            scores = _dot_nt(cm, bm)
            ps = []
            for hh in range(hpg):
                h = g * hpg + hh
                seg = cum[:, h:h + 1] - cum_t[h:h + 1, :]
                ps.append(scores * jnp.exp(jnp.where(tri, seg, -jnp.inf)))
            pcat = jnp.concatenate(ps, axis=1).astype(BF16)
            vg = vb[:, gl]
            vstack = jnp.concatenate([vg * hm_ref[hh] for hh in range(hpg)], axis=0)
            st = st_ref[s, g]
            yg = _dot(pcat, vstack) + _dot(cm, st.astype(BF16)) * ecum[:, gl]
            st_ref[s, g] = st * elast[:, gl] + _dot_tn(bm, wv[:, gl])
            ys.append(yg)
        y = jnp.concatenate(ys, axis=1) + dsk_ref[...] * xs
        y = y * z_ref[s, rows, :].astype(F32)
        outs = [_rms(y[:, g * 256:(g + 1) * 256], nw_ref[:, g * 256:(g + 1) * 256])
                for g in range(SSD_GROUPS)]
        o_ref[s, rows, :] = jnp.concatenate(outs, axis=1)

    def chunk(c, carry):
        rows = pl.ds(pl.multiple_of(c * CH, CH), CH)
        for s in range(NBP):
            one(s, rows)
        return carry

    lax.fori_loop(0, nch, chunk, 0)


def _ssd_call(proj3, tabs, dsk, nw, l, tt):
    tp = proj3.shape[1]
    ltri, expand, hm = tabs
    return pl.pallas_call(
        functools.partial(_ssd_kernel, nch=tt // CH),
        grid=(NB // NBP, tp // tt),
        in_specs=[_mixer_grp_spec(tt, G_Z), _mixer_grp_spec(tt, G_XS), _mixer_grp_spec(tt, G_BC),
                  _mixer_grp_spec(tt, G_DT),
                  _const_spec(ltri.shape), _const_spec(expand.shape), _const_spec(hm.shape),
                  _layer_spec((1, GW), l), _layer_spec((1, GW), l)],
        out_specs=pl.BlockSpec((NBP, tt, GW), lambda bb, i: (bb, i, 0)),
        out_shape=jax.ShapeDtypeStruct((NB, tp, GW), F32),
        scratch_shapes=[pltpu.VMEM((NBP, SSD_GROUPS, SSD_STATE, 256), F32)],
        compiler_params=_mixer_params(2 * NBP * tt * GW * (4 * 2 + 4) + (4 << 20)),
        name="ssd",
    )(proj3, proj3, proj3, proj3, ltri, expand, hm, dsk, nw)


HG_LEVELS = (64, 32, 16, 8, 4, 2)


def _hg_tables():
    t = np.arange(CH)
    blocks = []
    for c in HG_LEVELS:
        m = np.zeros((CH, CH), np.float32)
        for r in range(CH):
            start = (r // c) * c
            mid = start + c // 2 - 1
            if r <= mid:
                m[r, r + 1:mid + 1] = 1.0
            else:
                m[r, mid + 1:r + 1] = 1.0
        blocks.append(m)
    blocks.append((t[:, None] >= t[None, :]).astype(np.float32))
    blocks.append((t[None, :] > t[:, None]).astype(np.float32))
    tall = np.concatenate(blocks, axis=0)
    ones_blk = np.kron(np.eye(HG_HEADS, dtype=np.float32), np.ones((HG_DK, HG_DK), np.float32))
    pmask = np.kron(np.eye(2, dtype=np.float32), np.ones((128, 128), np.float32))
    return jnp.asarray(tall, BF16), jnp.asarray(ones_blk, BF16), jnp.asarray(pmask, F32)


def _hg_kernel(q_ref, lfh_ref, lfl_ref, k_ref, i_ref, g_ref, tall_ref, ones_ref, pmask_ref, nw_ref,
               o_ref, st_ref, *, nch):
    @pl.when(pl.program_id(1) == 0)
    def _():
        st_ref[...] = jnp.zeros_like(st_ref)

    row = lax.broadcasted_iota(jnp.int32, (CH, 1), 0)
    trow = lax.broadcasted_iota(jnp.int32, (CH, 128), 0)
    scol = lax.broadcasted_iota(jnp.int32, (CH, 128), 1) % CH
    zero128 = jnp.zeros((CH, 128), BF16)
    zero512 = jnp.zeros((CH, GW), BF16)

    def one(s, rows):
        qb = q_ref[s, rows, :]
        kb = k_ref[s, rows, :]
        vb = i_ref[s, rows, :]
        zall = _dot(tall_ref[...], lfh_ref[s, rows, :]) + _dot(tall_ref[...], lfl_ref[s, rows, :])
        nl = len(HG_LEVELS)
        cum = zall[nl * CH:(nl + 1) * CH, :]
        rem = zall[(nl + 1) * CH:(nl + 2) * CH, :]
        acc = [jnp.zeros((CH, 128), F32), jnp.zeros((CH, 128), F32)]
        for li, cs in enumerate(HG_LEVELS):
            fh = (row % cs) < (cs // 2)
            x = jnp.where(fh, kb, qb) * jnp.exp(zall[li * CH:(li + 1) * CH, :]).astype(BF16)
            qt = jnp.where(fh, zero512, x)
            kt = jnp.where(fh, x, zero512)
            lmask = (((trow // cs) == (scol // cs)) & ((trow % cs) >= (cs // 2))
                     & ((scol % cs) < (cs // 2))).astype(F32)
            for p in range(2):
                a0 = kt[:, (2 * p) * 128:(2 * p + 1) * 128]
                a1 = kt[:, (2 * p + 1) * 128:(2 * p + 2) * 128]
                kstack = jnp.concatenate([jnp.concatenate([a0, zero128], axis=1),
                                          jnp.concatenate([zero128, a1], axis=1)], axis=0)
                sc = _dot_nt(qt[:, p * 256:(p + 1) * 256], kstack)
                acc[p] = acc[p] + sc * lmask
        r_same = _dot(qb * kb, ones_ref[...])
        qd = qb * jnp.exp(cum).astype(BF16)
        kw = kb * jnp.exp(rem).astype(BF16)
        elast = jnp.exp(cum[CH - 1:CH, :])
        ys = []
        for p in range(2):
            pl_ = slice(p * 256, (p + 1) * 256)
            v0 = vb[:, (2 * p) * 128:(2 * p + 1) * 128]
            v1 = vb[:, (2 * p + 1) * 128:(2 * p + 2) * 128]
            vstack = jnp.concatenate([jnp.concatenate([v0, zero128], axis=1),
                                      jnp.concatenate([zero128, v1], axis=1)], axis=0)
            st = st_ref[s, p]
            yp = _dot(acc[p].astype(BF16), vstack) + _dot_nt(qd[:, pl_], st.astype(BF16))
            st_ref[s, p] = st * elast[:, pl_] + _dot_tn(vb[:, pl_], kw[:, pl_]) * pmask_ref[...]
            ys.append(yp)
        o = jnp.concatenate(ys, axis=1) + r_same * vb.astype(F32)
        outs = [_rms(o[:, h * 128:(h + 1) * 128], nw_ref[:, h * 128:(h + 1) * 128])
                for h in range(HG_HEADS)]
        o_ref[s, rows, :] = jnp.concatenate(outs, axis=1) * g_ref[s, rows, :].astype(F32)

    def chunk(c, carry):
        rows = pl.ds(pl.multiple_of(c * CH, CH), CH)
        for s in range(NBP):
            one(s, rows)
        return carry

    lax.fori_loop(0, nch, chunk, 0)


def _hg_call(proj3, lfl3, hk3, tabs, nw, l, tt):
    tp = proj3.shape[1]
    tall, ones_blk, pmask = tabs
    xspec = pl.BlockSpec((NBP, tt, GW), lambda bb, i: (bb, i, 0))
    return pl.pallas_call(
        functools.partial(_hg_kernel, nch=tt // CH),
        grid=(NB // NBP, tp // tt),
        in_specs=[_mixer_grp_spec(tt, G_HQ), _mixer_grp_spec(tt, G_HF), xspec, xspec,
                  _mixer_grp_spec(tt, G_HI), _mixer_grp_spec(tt, G_HG),
                  _const_spec(tall.shape), _const_spec(ones_blk.shape), _const_spec(pmask.shape),
                  _layer_spec((1, GW), l)],
        out_specs=pl.BlockSpec((NBP, tt, GW), lambda bb, i: (bb, i, 0)),
        out_shape=jax.ShapeDtypeStruct((NB, tp, GW), F32),
        scratch_shapes=[pltpu.VMEM((NBP, 2, 256, 256), F32)],
        compiler_params=_mixer_params(2 * NBP * tt * GW * (6 * 2 + 4) + (4 << 20)),
        name="hgrn2",
    )(proj3, proj3, lfl3, hk3, proj3, proj3, tall, ones_blk, pmask, nw)


def _merge_kernel(h_ref, y0_ref, y1_ref, y2_ref, y3_ref, npre_ref, wg_ref, wb_ref, wo_ref,
                  npost_ref, o_ref, *, tm, tps):
    h = h_ref[...]
    u = _rms(h, npre_ref[...]).astype(BF16)
    mixed = jnp.zeros((tm, D_MODEL), F32)
    for n, y_ref in enumerate((y0_ref, y1_ref, y2_ref, y3_ref)):
        gate = _sigmoid(_dot(u, wg_ref[:, n * D_MODEL:(n + 1) * D_MODEL]))
        mixed = mixed + gate * _dot(y_ref[...].astype(BF16), wb_ref[n])
    out = _dot(mixed.astype(BF16), wo_ref[...])
    valid = _valid_rows(pl.program_id(0), tps, tm)
    o_ref[...] = jnp.where(valid, h + _rms(out, npost_ref[...]), 0.0)


def _merge_call(h, ys, npre, wg, wb, wo, npost, l, tm, tps):
    rows = h.shape[0]
    nbytes = (2 * (2 * tm * D_MODEL * 4 + 4 * tm * GW * 4)
              + (D_MODEL * 4 * D_MODEL + 4 * GW * D_MODEL + D_MODEL * D_MODEL) * 2
              + 6 * tm * D_MODEL * 4)
    yspec = pl.BlockSpec((tm, GW), lambda i: (i, 0))
    return pl.pallas_call(
        functools.partial(_merge_kernel, tm=tm, tps=tps),
        grid=(rows // tm,),
        in_specs=[pl.BlockSpec((tm, D_MODEL), lambda i: (i, 0)), yspec, yspec, yspec, yspec,
                  _layer_spec((1, D_MODEL), l), _layer_spec((D_MODEL, 4 * D_MODEL), l),
                  _layer_spec((N_BRANCH, GW, D_MODEL), l), _layer_spec((D_MODEL, D_MODEL), l),
                  _layer_spec((1, D_MODEL), l)],
        out_specs=pl.BlockSpec((tm, D_MODEL), lambda i: (i, 0)),
        out_shape=jax.ShapeDtypeStruct((rows, D_MODEL), F32),
        compiler_params=pltpu.CompilerParams(
            dimension_semantics=("parallel",), vmem_limit_bytes=_vmem_limit(nbytes)),
        name="merge",
    )(h, *ys, npre, wg, wb, wo, npost)


def _mlp_kernel(h_ref, npre_ref, wu_ref, wd_ref, npost_ref, o_ref, *, tm, tps, fc):
    h = h_ref[...]
    m = _rms(h, npre_ref[...]).astype(BF16)
    acc = jnp.zeros((tm, D_MODEL), F32)
    for c in range(D_FF // fc):
        a = jnp.maximum(_dot(m, wu_ref[:, c * fc:(c + 1) * fc]), 0.0)
        acc = acc + _dot((a * a).astype(BF16), wd_ref[c * fc:(c + 1) * fc, :])
    valid = _valid_rows(pl.program_id(0), tps, tm)
    o_ref[...] = jnp.where(valid, h + _rms(acc, npost_ref[...]), 0.0)


def _mlp_call(h, npre, wu, wd, npost, l, tm, tps):
    rows = h.shape[0]
    nbytes = 2 * (2 * tm * D_MODEL * 4) + (2 * D_MODEL * D_FF * 2) + 6 * tm * D_MODEL * 4
    return pl.pallas_call(
        functools.partial(_mlp_kernel, tm=tm, tps=tps, fc=1024),
        grid=(rows // tm,),
        in_specs=[pl.BlockSpec((tm, D_MODEL), lambda i: (i, 0)), _layer_spec((1, D_MODEL), l),
                  _layer_spec((D_MODEL, D_FF), l), _layer_spec((D_FF, D_MODEL), l),
                  _layer_spec((1, D_MODEL), l)],
        out_specs=pl.BlockSpec((tm, D_MODEL), lambda i: (i, 0)),
        out_shape=jax.ShapeDtypeStruct((rows, D_MODEL), F32),
        compiler_params=pltpu.CompilerParams(
            dimension_semantics=("parallel",), vmem_limit_bytes=_vmem_limit(nbytes)),
        name="mlp",
    )(h, npre, wu, wd, npost)


def _prep_w_in(w_in):
    depth = w_in.shape[0]
    perm = np.zeros(256, np.int64)
    for half in range(2):
        for h in range(RET_HEADS):
            for i in range(32):
                perm[half * 128 + h * 32 + i] = h * 64 + half * 32 + i
    rq = w_in[:, :, 0:256][:, :, perm]
    rk = w_in[:, :, 256:512][:, :, perm]
    rest = w_in[:, :, 512:3584]
    hg = w_in[:, :, 3592:5640]
    dt = jnp.pad(w_in[:, :, 3584:3592], ((0, 0), (0, 0), (0, GW - SSD_HEADS)))
    wa = jnp.concatenate([rq, rk, rest, hg, dt], axis=2).astype(BF16)
    wg = w_in[:, :, 5640:].astype(BF16)
    assert wa.shape == (depth, D_MODEL, NA)
    return wa, wg


def _prep_s5(lam_re, lam_im, b_re, b_im, c_re, c_im, log_step):
    depth = lam_re.shape[0]
    step = jnp.exp(log_step.astype(F32))[..., None]
    lr, li = lam_re.astype(F32), lam_im.astype(F32)
    mag = jnp.exp(lr * step)
    ab_re, ab_im = mag * jnp.cos(li * step), mag * jnp.sin(li * step)
    inv = 1.0 / (lr * lr + li * li)
    co_re = ((ab_re - 1.0) * lr + ab_im * li) * inv
    co_im = (ab_im * lr - (ab_re - 1.0) * li) * inv
    br, bi = b_re.astype(F32), b_im.astype(F32)
    bb_re = co_re[..., None] * br - co_im[..., None] * bi
    bb_im = co_re[..., None] * bi + co_im[..., None] * br
    eye = jnp.eye(S5_GROUPS, dtype=F32)
    gps = S5_GROUPS // S5_SG
    eye5 = eye[None, :, None, :, None]
    bd = jnp.stack([m.transpose(0, 1, 3, 2)[:, :, :, None, :] * eye5 for m in (bb_re, bb_im)], axis=3)
    bd = bd.reshape(depth, GW, 2, S5_SG, gps, S5_STATE).transpose(0, 1, 3, 2, 4, 5)
    bd = bd.reshape(depth, GW, 2 * S5_P2)
    n_tiles = 2 * S5_P2 // MXU_T
    bm = bd.reshape(depth, GW // MXU_T, MXU_T, n_tiles, MXU_T).sum(axis=1)
    bm = bm.transpose(0, 2, 1, 3).astype(BF16)
    c_parts = (c_re.astype(F32), c_im.astype(F32) * (-1.0))
    cd = jnp.stack([m.transpose(0, 1, 3, 2)[:, :, :, None, :] * eye5
                    for m in c_parts], axis=1)
    cd = cd.reshape(depth, 2, S5_SG, gps, S5_STATE, GW).transpose(0, 2, 1, 3, 4, 5)
    cd = cd.reshape(depth, S5_SG, S5_SGW, GW)
    cm = cd.reshape(depth, S5_SG, S5_SGW, GW // MXU_T, MXU_T).sum(axis=3).astype(BF16)
    ar = jnp.broadcast_to(ab_re.reshape(depth, 1, S5_P2), (depth, NB, S5_P2))
    ai = jnp.broadcast_to(ab_im.reshape(depth, 1, S5_P2), (depth, NB, S5_P2))
    return bm, ar, ai, cm


def _rope_tables(tp):
    half = RET_DK // 2
    inv_freq = RET_ROPE_BASE ** (-jnp.arange(half, dtype=F32) / half)
    pos = jnp.maximum(jnp.arange(tp, dtype=F32) - PADF, 0.0)
    ang = pos[:, None] * inv_freq[None, :]
    return jnp.tile(jnp.cos(ang), (1, RET_HEADS)), jnp.tile(jnp.sin(ang), (1, RET_HEADS))


def _pick_tiles(tp):
    tt = 192 if tp % 192 == 0 else CH
    tm = next(c for c in (704, 192, CH) if tp % c == 0)
    return tt, tm


def kernel(x, meta_tokens, w_in, w_branch, w_out, norm_pre_mix, norm_post_mix, norm_pre_mlp, norm_post_mlp, w_up, w_down, ret_gn_w, s5_lam_re, s5_lam_im, s5_b_re, s5_b_im, s5_c_re, s5_c_im, s5_d, s5_log_step, s5_glu_w, s5_glu_b, ssd_conv_w, ssd_conv_b, ssd_dt_bias, ssd_a_log, ssd_d, ssd_norm_w, hgrn_lb, hgrn_norm_w):
    bsz, seq, _ = x.shape
    depth = w_in.shape[0]
    assert bsz == NB and x.shape[2] == D_MODEL
    tp = PADF + N_META + seq
    assert tp % CH == 0
    tt, tm = _pick_tiles(tp)
    rows = tp * NB

    meta = jnp.broadcast_to(meta_tokens[None].astype(x.dtype), (bsz, N_META, D_MODEL))
    h = jnp.concatenate([jnp.zeros((bsz, PADF, D_MODEL), x.dtype), meta, x], axis=1)
    h = h.reshape(rows, D_MODEL).astype(F32)

    wa, wg = _prep_w_in(w_in)
    wb = w_branch.astype(BF16)
    wo = w_out.astype(BF16)
    wu = w_up.astype(BF16)
    wd = w_down.astype(BF16)
    r3 = lambda a: a.astype(F32).reshape(depth, 1, -1)
    npre, npost, mpre, mpost = r3(norm_pre_mix), r3(norm_post_mix), r3(norm_pre_mlp), r3(norm_post_mlp)
    bm, ar, ai, cm = _prep_s5(s5_lam_re, s5_lam_im, s5_b_re, s5_b_im, s5_c_re, s5_c_im, s5_log_step)
    s5gw = s5_glu_w.astype(BF16)
    s5gb = r3(s5_glu_b)
    s5d = r3(s5_d)
    zc = jnp.zeros((depth, SSD_CONV, GW), F32)
    cw = jnp.concatenate([zc] * G_XS + [ssd_conv_w[:, :, :GW].astype(F32), ssd_conv_w[:, :, GW:].astype(F32)]
                         + [zc] * (N_GROUPS - G_BC - 1), axis=2)
    zb = jnp.zeros((depth, 1, GW), F32)
    cb = jnp.concatenate([zb] * G_XS + [r3(ssd_conv_b[:, :GW]), r3(ssd_conv_b[:, GW:])]
                         + [zb] * (N_GROUPS - G_BC - 1), axis=2)
    pad8 = lambda a: jnp.pad(a.astype(F32), ((0, 0), (0, 128 - SSD_HEADS))).reshape(depth, 1, 128)
    dtb = pad8(ssd_dt_bias)
    ssd_a = pad8(-jnp.exp(ssd_a_log.astype(F32)))
    ssd_dsk = r3(jnp.repeat(ssd_d.astype(F32), SSD_HEADDIM, axis=1))
    ssd_nw = r3(ssd_norm_w)
    lb_all = jnp.cumsum(jax.nn.softmax(hgrn_lb.astype(F32), axis=0), axis=0)
    lb_all = r3(lb_all - lb_all[0])
    hg_nw = r3(hgrn_norm_w)
    gnw = r3(ret_gn_w)

    cos_t, sin_t = _rope_tables(tp)
    ret_tabs, ssd_tabs, hg_tabs = _ret_tables(), _ssd_tables(), _hg_tables()

    flat = lambda y: y.reshape(rows, GW)
    v3 = lambda y: y.reshape(NB, tp, y.shape[-1])
    for l in range(depth):
        proj, lfl, hk = _inproj(h, npre, wa, cos_t, sin_t, cw, cb, lb_all, dtb, ssd_a, l, tp)
        proj3 = v3(proj)
        y_ret = _ret_call(proj3, ret_tabs, gnw, l, tt)
        y_s5 = _s5_call(proj3, bm, ar, ai, cm, s5d, s5gw, s5gb, l, CH)
        y_ssd = _ssd_call(proj3, ssd_tabs, ssd_dsk, ssd_nw, l, tt)
        y_hg = _hg_call(proj3, v3(lfl), v3(hk), hg_tabs, hg_nw, l, tt)
        h = _merge_call(h, (flat(y_ret), flat(y_s5), flat(y_ssd), flat(y_hg)),
                        npre, wg, wb, wo, npost, l, tm, tp // tm)
        h = _mlp_call(h, mpre, wu, wd, mpost, l, tm, tp // tm)

    return h.reshape(NB, tp, D_MODEL)[:, PADF + N_META:].astype(x.dtype)
```

```python
import functools

import numpy as np
import jax
import jax.numpy as jnp
from jax import lax
from jax.experimental import pallas as pl
from jax.experimental.pallas import tpu as pltpu

F32 = jnp.float32
BF16 = jnp.bfloat16

D_MODEL = 1024
N_META = 16
BRANCH_WIDTH = 512
N_BRANCH = 4
D_FF = 4 * D_MODEL
NORM_EPS = 1e-6
RET_HEADS = 4
RET_DK = 64
RET_DV = 128
RET_ROPE_BASE = 10000.0
S5_GROUP = 16
S5_GROUPS = 32
S5_STATE = 64
SSD_HEADDIM = 64
SSD_HEADS = 8
SSD_GROUPS = 2
SSD_STATE = 128
SSD_CONV = 4
HG_HEADS = 4
HG_DK = 128

NB = 8
PADF = 112
CH = 128
S5_TT = 64
GW = 512
N_GROUPS = 12
NA = GW * N_GROUPS
G_QK, G_RV, G_RG, G_S5, G_Z, G_XS, G_BC, G_HQ, G_HF, G_HI, G_HG, G_DT = range(N_GROUPS)
S5_P2 = S5_GROUPS * S5_STATE
S5_SG = 4
S5_SGW = 2 * S5_P2 // S5_SG
MXU_T = 256
NBP = 8
VMEM_LIMIT_CAP = 56 * 1024 * 1024


def _vmem_limit(nbytes):
    return int(min(VMEM_LIMIT_CAP, max(32 * 1024 * 1024, nbytes * 5 // 4 + 8 * 1024 * 1024)))


def _const_spec(shape):
    nd = len(shape)
    return pl.BlockSpec(shape, lambda *a: (0,) * nd, pipeline_mode=pl.Buffered(1))


def _layer_spec(shape, l):
    nd = len(shape)
    return pl.BlockSpec((None,) + tuple(shape), lambda *a: (l,) + (0,) * nd,
                        pipeline_mode=pl.Buffered(1))


def _split_bf16(x):
    hi = x.astype(BF16)
    lo = (x - hi.astype(F32)).astype(BF16)
    return hi, lo


def _dot(a, b):
    return jnp.dot(a, b, preferred_element_type=F32)


def _dot_nt(a, b):
    return lax.dot_general(a, b, (((1,), (1,)), ((), ())), preferred_element_type=F32)


def _dot_tn(a, b):
    return lax.dot_general(a, b, (((0,), (0,)), ((), ())), preferred_element_type=F32)


def _dot_sel(x, e_bf16):
    hi, lo = _split_bf16(x)
    return _dot(hi, e_bf16) + _dot(lo, e_bf16)


def _sigmoid(x):
    return 1.0 / (1.0 + jnp.exp(-x))


def _silu(x):
    return x * _sigmoid(x)


def _rms(x, w):
    ms = jnp.mean(x * x, axis=-1, keepdims=True)
    return x * lax.rsqrt(ms + NORM_EPS) * w


def _valid_rows(tile_idx, tiles_per_seq, tm):
    tpos = (tile_idx % tiles_per_seq) * tm + lax.broadcasted_iota(jnp.int32, (tm, 1), 0)
    return tpos >= PADF


def _assemble_kernel(x_ref, meta_ref, o_ref):
    o_ref[0:PADF, :] = jnp.zeros((PADF, D_MODEL), F32)
    o_ref[PADF:PADF + N_META, :] = meta_ref[...].astype(F32)
    o_ref[PADF + N_META:, :] = x_ref[...].astype(F32)


def _assemble(x, meta_tokens):
    bsz, seq, _ = x.shape
    tp = PADF + N_META + seq
    return pl.pallas_call(
        _assemble_kernel,
        grid=(bsz,),
        in_specs=[pl.BlockSpec((seq, D_MODEL), lambda b: (b, 0)), _const_spec((N_META, D_MODEL))],
        out_specs=pl.BlockSpec((tp, D_MODEL), lambda b: (b, 0)),
        out_shape=jax.ShapeDtypeStruct((bsz * tp, D_MODEL), F32),
        compiler_params=pltpu.CompilerParams(
            dimension_semantics=("parallel",),
            vmem_limit_bytes=_vmem_limit(2 * (seq + tp) * D_MODEL * 4)),
        name="assemble",
    )(x.reshape(bsz * seq, D_MODEL), meta_tokens)


def _inproj_kernel(h_ref, nw_ref, w_ref, cos_ref, sin_ref, cw_ref, cb_ref, lb_ref, dtb_ref, a_ref,
                   o_ref, lfl_ref, hk_ref, xn_ref, *, tm):
    j = pl.program_id(1)

    @pl.when(j == 0)
    def _():
        xn_ref[...] = _rms(h_ref[...], nw_ref[...]).astype(BF16)

    def mm():
        return _dot(xn_ref[...], w_ref[...])

    @pl.when((j == G_RV) | (j == G_S5) | (j == G_HI))
    def _():
        o_ref[...] = mm().astype(BF16)

    @pl.when((j == G_RG) | (j == G_Z) | (j == G_HQ) | (j == G_HG))
    def _():
        o_ref[...] = _silu(mm()).astype(BF16)

    @pl.when((j == G_XS) | (j == G_BC))
    def _():
        r = mm()
        acc = cb_ref[...] + r * cw_ref[SSD_CONV - 1:SSD_CONV, :]
        for back in range(1, SSD_CONV):
            tap = SSD_CONV - 1 - back
            acc = acc + pltpu.roll(r, back, 0) * cw_ref[tap:tap + 1, :]
        valid = _valid_rows(0, 1, tm)
        o_ref[...] = jnp.where(valid, _silu(acc), 0.0).astype(BF16)

    @pl.when(j == G_QK)
    def _():
        r = mm()
        cs = cos_ref[...]
        sn = sin_ref[...]
        q1, q2, k1, k2 = r[:, 0:128], r[:, 128:256], r[:, 256:384], r[:, 384:512]
        o_ref[:, 0:128] = (q1 * cs - q2 * sn).astype(BF16)
        o_ref[:, 128:256] = (q1 * sn + q2 * cs).astype(BF16)
        o_ref[:, 256:384] = ((k1 * cs - k2 * sn) * (RET_DK ** -0.5)).astype(BF16)
        o_ref[:, 384:512] = ((k1 * sn + k2 * cs) * (RET_DK ** -0.5)).astype(BF16)

    @pl.when(j == G_HF)
    def _():
        lb = lb_ref[...]
        f = lb + (1.0 - lb) * _sigmoid(mm())
        hi, lo = _split_bf16(jnp.log(f))
        o_ref[...] = hi
        lfl_ref[...] = lo
        hk_ref[...] = (1.0 - f).astype(BF16)

    @pl.when(j == G_DT)
    def _():
        dt = jax.nn.softplus(mm()[:, 0:128] + dtb_ref[...])
        dhi, dlo = _split_bf16(dt)
        lhi, llo = _split_bf16(dt * a_ref[...])
        o_ref[:, 0:128] = dhi
        o_ref[:, 128:256] = dlo
        o_ref[:, 256:384] = lhi
        o_ref[:, 384:512] = llo


def _inproj(h, nw, w, cos_t, sin_t, cw, cb, lb, dtb, a, l, tm):
    rows = h.shape[0]
    assert cos_t.shape[0] == tm
    nbytes = (2 * tm * D_MODEL * 4 + tm * D_MODEL * 2 + 2 * D_MODEL * GW * 2 + 3 * 2 * tm * GW * 2
              + 2 * tm * 128 * 4 + 4 * tm * GW * 4)
    ospec0 = pl.BlockSpec((tm, GW), lambda i, j: (i, 0))
    return pl.pallas_call(
        functools.partial(_inproj_kernel, tm=tm),
        grid=(rows // tm, N_GROUPS),
        in_specs=[
            pl.BlockSpec((tm, D_MODEL), lambda i, j: (i, 0)),
            pl.BlockSpec((None, 1, D_MODEL), lambda i, j: (l, 0, 0)),
            pl.BlockSpec((None, D_MODEL, GW), lambda i, j: (l, 0, j)),
            _const_spec((tm, 128)), _const_spec((tm, 128)),
            pl.BlockSpec((None, SSD_CONV, GW), lambda i, j: (l, 0, j)),
            pl.BlockSpec((None, 1, GW), lambda i, j: (l, 0, j)),
            _layer_spec((1, GW), l), _layer_spec((1, 128), l), _layer_spec((1, 128), l),
        ],
        out_specs=[pl.BlockSpec((tm, GW), lambda i, j: (i, j)), ospec0, ospec0],
        out_shape=[jax.ShapeDtypeStruct((rows, NA), BF16), jax.ShapeDtypeStruct((rows, GW), BF16),
                   jax.ShapeDtypeStruct((rows, GW), BF16)],
        scratch_shapes=[pltpu.VMEM((tm, D_MODEL), BF16)],
        compiler_params=pltpu.CompilerParams(
            dimension_semantics=("parallel", "arbitrary"), vmem_limit_bytes=_vmem_limit(nbytes)),
        name="inproj",
    )(h, nw, w, cos_t, sin_t, cw, cb, lb, dtb, a)


def _mixer_grp_spec(tt, g):
    return pl.BlockSpec((NBP, tt, GW), lambda bb, i, g=g: (bb, i, g))


def _mixer_params(est_bytes):
    return pltpu.CompilerParams(dimension_semantics=("parallel", "arbitrary"),
                                vmem_limit_bytes=_vmem_limit(est_bytes))


def _ret_tables():
    gam = 1.0 - np.exp2(-5.0 - np.arange(RET_HEADS, dtype=np.float64))
    lg = np.log(gam)
    t = np.arange(CH)
    col_head = (np.arange(2 * 128) % 128) // 32
    dall = np.zeros((CH, RET_HEADS * CH), np.float64)
    for h in range(RET_HEADS):
        d = np.exp(lg[h] * (t[:, None] - t[None, :]))
        dall[:, h * CH:(h + 1) * CH] = np.where(t[:, None] >= t[None, :], d, 0.0)
    qdec = np.exp(lg[col_head][None, :] * (t[:, None] + 1.0))
    kdec = np.exp(lg[col_head][None, :] * (CH - 1.0 - t[:, None]))
    vcol_head = np.arange(RET_HEADS * RET_DV) // RET_DV
    sdec = np.exp(lg[col_head] * CH)[:, None] * np.ones((1, 512))
    smask = (col_head[:, None] == vcol_head[None, :]).astype(np.float64)
    kmask = (col_head[None, :] == np.arange(RET_HEADS)[:, None]).astype(np.float64)[:, None, :]
    vmask = (vcol_head[None, :] == np.arange(RET_HEADS)[:, None]).astype(np.float64)[:, None, :]
    f = lambda a: jnp.asarray(a, F32)
    b = lambda a: jnp.asarray(a, BF16)
    return f(dall), f(qdec), f(kdec), f(sdec), f(smask), b(kmask), b(vmask)


def _ret_kernel(qk_ref, v_ref, g_ref, dall_ref, qdec_ref, kdec_ref, sdec_ref,
                smask_ref, kmask_ref, vmask_ref, gnw_ref, o_ref, st_ref, *, nch):
    @pl.when(pl.program_id(1) == 0)
    def _():
        st_ref[...] = jnp.zeros_like(st_ref)

    def one(s, rows):
        qk = qk_ref[s, rows, :]
        qb = qk[:, 0:256]
        kb = qk[:, 256:512]
        vb = v_ref[s, rows, :]
        kstack = jnp.concatenate([kb * kmask_ref[h] for h in range(RET_HEADS)], axis=0)
        s_all = _dot_nt(qb, kstack)
        p = (s_all * dall_ref[...]).astype(BF16)
        vstack = jnp.concatenate([vb * vmask_ref[h] for h in range(RET_HEADS)], axis=0)
        y = _dot(p, vstack)
        st = st_ref[s]
        y = y + _dot((qb.astype(F32) * qdec_ref[...]).astype(BF16), st.astype(BF16))
        new = _dot_tn((kb.astype(F32) * kdec_ref[...]).astype(BF16), vb)
        st_ref[s] = st * sdec_ref[...] + new * smask_ref[...]
        outs = []
        for h in range(RET_HEADS):
            yh = y[:, h * RET_DV:(h + 1) * RET_DV]
            mu = jnp.mean(yh, axis=-1, keepdims=True)
            d = yh - mu
            var = jnp.mean(d * d, axis=-1, keepdims=True)
            outs.append(d * lax.rsqrt(var + NORM_EPS))
        yn = jnp.concatenate(outs, axis=1) * gnw_ref[...]
        o_ref[s, rows, :] = g_ref[s, rows, :].astype(F32) * yn

    def chunk(c, carry):
        rows = pl.ds(pl.multiple_of(c * CH, CH), CH)
        for s in range(NBP):
            one(s, rows)
        return carry

    lax.fori_loop(0, nch, chunk, 0)


def _ret_call(proj3, tabs, gnw, l, tt):
    tp = proj3.shape[1]
    dall, qdec, kdec, sdec, smask, kmask, vmask = tabs
    return pl.pallas_call(
        functools.partial(_ret_kernel, nch=tt // CH),
        grid=(NB // NBP, tp // tt),
        in_specs=[_mixer_grp_spec(tt, G_QK), _mixer_grp_spec(tt, G_RV), _mixer_grp_spec(tt, G_RG),
                  _const_spec(dall.shape), _const_spec(qdec.shape), _const_spec(kdec.shape),
                  _const_spec(sdec.shape), _const_spec(smask.shape), _const_spec(kmask.shape),
                  _const_spec(vmask.shape), _layer_spec((1, GW), l)],
        out_specs=pl.BlockSpec((NBP, tt, GW), lambda bb, i: (bb, i, 0)),
        out_shape=jax.ShapeDtypeStruct((NB, tp, GW), F32),
        scratch_shapes=[pltpu.VMEM((NBP, 256, 512), F32)],
        compiler_params=_mixer_params(2 * NBP * tt * GW * (3 * 2 + 4) + (4 << 20)),
        name="retention",
    )(proj3, proj3, proj3, dall, qdec, kdec, sdec, smask, kmask, vmask, gnw)


def _s5_kernel(u_ref, bm_ref, ar_ref, ai_ref, cm_ref, d_ref, gw_ref, gb_ref, o_ref,
               utm_ref, ytm_ref, bu_ref, xs_ref, *, tt):
    @pl.when(pl.program_id(0) == 0)
    def _():
        xs_ref[...] = jnp.zeros_like(xs_ref)

    for b in range(NB):
        utm_ref[:, b, :] = u_ref[b].astype(F32)
    u = utm_ref[...].reshape(tt * NB, GW)
    ub = u.astype(BF16)
    half = S5_SGW // 2
    for nt in range(2 * S5_P2 // MXU_T):
        kb = nt // (2 * S5_P2 // MXU_T // (GW // MXU_T))
        bu_ref[:, nt * MXU_T:(nt + 1) * MXU_T] = _dot(ub[:, kb * MXU_T:(kb + 1) * MXU_T], bm_ref[nt])
    for c in range(S5_SG):
        cr = pl.ds(c * S5_SGW, half)
        ci = pl.ds(c * S5_SGW + half, half)
        ar = ar_ref[:, pl.ds(c * half, half)]
        ai = ai_ref[:, pl.ds(c * half, half)]

        def step(t, carry):
            xr, xi = carry
            rows = pl.ds(pl.multiple_of(t * NB, NB), NB)
            nr = ar * xr - ai * xi + bu_ref[rows, cr]
            ni = ar * xi + ai * xr + bu_ref[rows, ci]
            bu_ref[rows, cr] = nr
            bu_ref[rows, ci] = ni
            return nr, ni

        xr, xi = lax.fori_loop(0, tt, step, (xs_ref[:, cr], xs_ref[:, ci]), unroll=True)
        xs_ref[:, cr] = xr
        xs_ref[:, ci] = xi
    ys = []
    for j in range(GW // MXU_T):
        acc = None
        for c in range(j * (S5_SG // 2), (j + 1) * (S5_SG // 2)):
            part = _dot(bu_ref[:, c * S5_SGW:(c + 1) * S5_SGW].astype(BF16), cm_ref[c])
            acc = part if acc is None else acc + part
        ys.append(acc)
    y = jnp.concatenate(ys, axis=1) + d_ref[...] * u
    y = jax.nn.gelu(y)
    ag = _dot(y.astype(BF16), gw_ref[...]) + gb_ref[...]
    ytm_ref[...] = (ag[:, :BRANCH_WIDTH] * _sigmoid(ag[:, BRANCH_WIDTH:])).reshape(tt, NB, GW)
    for b in range(NB):
        o_ref[b] = ytm_ref[:, b, :]


def _s5_call(proj3, bm, ar, ai, cm, dsk, gw, gb, l, tt):
    tp = proj3.shape[1]
    rb = tt * NB
    nbytes = (2 * rb * GW * 2 + 2 * rb * GW * 4 + 2 * rb * GW * 4 + rb * 2 * S5_P2 * 4 * 2
              + (bm.shape[1] * MXU_T * MXU_T + cm.shape[1] * S5_SGW * MXU_T + GW * 2 * GW) * 2)
    return pl.pallas_call(
        functools.partial(_s5_kernel, tt=tt),
        grid=(tp // tt,),
        in_specs=[pl.BlockSpec((NB, tt, GW), lambda i: (0, i, G_S5)),
                  _layer_spec(bm.shape[1:], l), _layer_spec((NB, S5_P2), l),
                  _layer_spec((NB, S5_P2), l), _layer_spec(cm.shape[1:], l),
                  _layer_spec((1, GW), l), _layer_spec((GW, 2 * GW), l), _layer_spec((1, 2 * GW), l)],
        out_specs=pl.BlockSpec((NB, tt, GW), lambda i: (0, i, 0)),
        out_shape=jax.ShapeDtypeStruct((NB, tp, GW), F32),
        scratch_shapes=[pltpu.VMEM((tt, NB, GW), F32), pltpu.VMEM((tt, NB, GW), F32),
                        pltpu.VMEM((rb, 2 * S5_P2), F32), pltpu.VMEM((NB, 2 * S5_P2), F32)],
        compiler_params=pltpu.CompilerParams(
            dimension_semantics=("arbitrary",), vmem_limit_bytes=_vmem_limit(nbytes)),
        name="s5",
    )(proj3, bm, ar, ai, cm, dsk, gw, gb)


def _ssd_tables():
    t = np.arange(CH)
    ltri = (t[:, None] >= t[None, :]).astype(np.float32)
    expand = np.zeros((128, GW), np.float32)
    for h in range(SSD_HEADS):
        expand[h, h * SSD_HEADDIM:(h + 1) * SSD_HEADDIM] = 1.0
    hm = np.zeros((4, 1, 256), np.float32)
    for h in range(4):
        hm[h, 0, h * SSD_HEADDIM:(h + 1) * SSD_HEADDIM] = 1.0
    return jnp.asarray(ltri, BF16), jnp.asarray(expand, BF16), jnp.asarray(hm, BF16)


def _ssd_kernel(z_ref, xs_ref, bc_ref, dt_ref, ltri_ref, exp_ref, hm_ref, dsk_ref, nw_ref, o_ref,
                st_ref, *, nch):
    @pl.when(pl.program_id(1) == 0)
    def _():
        st_ref[...] = jnp.zeros_like(st_ref)

    tri = lax.broadcasted_iota(jnp.int32, (CH, CH), 0) >= lax.broadcasted_iota(jnp.int32, (CH, CH), 1)
    hpg = SSD_HEADS // SSD_GROUPS

    def one(s, rows):
        xs = xs_ref[s, rows, :].astype(F32)
        bc = bc_ref[s, rows, :]
        dtg = dt_ref[s, rows, :]
        cum = _dot(ltri_ref[...], dtg[:, 256:384]) + _dot(ltri_ref[...], dtg[:, 384:512])
        dt_e = _dot(dtg[:, 0:128], exp_ref[...]) + _dot(dtg[:, 128:256], exp_ref[...])
        cum_e = _dot_sel(cum, exp_ref[...])
        last_e = cum_e[CH - 1:CH, :]
        v = xs * dt_e
        vb = v.astype(BF16)
        wv = (v * jnp.exp(last_e - cum_e)).astype(BF16)
        ecum = jnp.exp(cum_e)
        elast = jnp.exp(last_e)
        cum_t = cum.T
        ys = []
        for g in range(SSD_GROUPS):
            gl = slice(g * 256, (g + 1) * 256)
            bm = bc[:, g * 128:(g + 1) * 128]
            cm = bc[:, 256 + g * 128:256 + (g + 1) * 128]
            scores = _dot_nt(cm, bm)
            ps = []
            for hh in range(hpg):
                h = g * hpg + hh
                seg = cum[:, h:h + 1] - cum_t[h:h + 1, :]
                ps.append(scores * jnp.exp(jnp.where(tri, seg, -jnp.inf)))
            pcat = jnp.concatenate(ps, axis=1).astype(BF16)
            vg = vb[:, gl]
            vstack = jnp.concatenate([vg * hm_ref[hh] for hh in range(hpg)], axis=0)
            st = st_ref[s, g]
            yg = _dot(pcat, vstack) + _dot(cm, st.astype(BF16)) * ecum[:, gl]
            st_ref[s, g] = st * elast[:, gl] + _dot_tn(bm, wv[:, gl])
            ys.append(yg)
        y = jnp.concatenate(ys, axis=1) + dsk_ref[...] * xs
        y = y * z_ref[s, rows, :].astype(F32)
        outs = [_rms(y[:, g * 256:(g + 1) * 256], nw_ref[:, g * 256:(g + 1) * 256])
                for g in range(SSD_GROUPS)]
        o_ref[s, rows, :] = jnp.concatenate(outs, axis=1)

    def chunk(c, carry):
        rows = pl.ds(pl.multiple_of(c * CH, CH), CH)
        for s in range(NBP):
            one(s, rows)
        return carry

    lax.fori_loop(0, nch, chunk, 0)


def _ssd_call(proj3, tabs, dsk, nw, l, tt):
    tp = proj3.shape[1]
    ltri, expand, hm = tabs
    return pl.pallas_call(
        functools.partial(_ssd_kernel, nch=tt // CH),
        grid=(NB // NBP, tp // tt),
        in_specs=[_mixer_grp_spec(tt, G_Z), _mixer_grp_spec(tt, G_XS), _mixer_grp_spec(tt, G_BC),
                  _mixer_grp_spec(tt, G_DT),
                  _const_spec(ltri.shape), _const_spec(expand.shape), _const_spec(hm.shape),
                  _layer_spec((1, GW), l), _layer_spec((1, GW), l)],
        out_specs=pl.BlockSpec((NBP, tt, GW), lambda bb, i: (bb, i, 0)),
        out_shape=jax.ShapeDtypeStruct((NB, tp, GW), F32),
        scratch_shapes=[pltpu.VMEM((NBP, SSD_GROUPS, SSD_STATE, 256), F32)],
        compiler_params=_mixer_params(2 * NBP * tt * GW * (4 * 2 + 4) + (4 << 20)),
        name="ssd",
    )(proj3, proj3, proj3, proj3, ltri, expand, hm, dsk, nw)


HG_LEVELS = tuple(CH >> i for i in range(CH.bit_length() - 1))


def _hg_tables():
    t = np.arange(CH)
    blocks = []
    for c in HG_LEVELS:
        m = np.zeros((CH, CH), np.float32)
        for r in range(CH):
            start = (r // c) * c
            mid = start + c // 2 - 1
            if r <= mid:
                m[r, r + 1:mid + 1] = 1.0
            else:
                m[r, mid + 1:r + 1] = 1.0
        blocks.append(m)
    blocks.append((t[:, None] >= t[None, :]).astype(np.float32))
    blocks.append((t[None, :] > t[:, None]).astype(np.float32))
    tall = np.concatenate(blocks, axis=0)
    ones_blk = np.kron(np.eye(HG_HEADS, dtype=np.float32), np.ones((HG_DK, HG_DK), np.float32))
    pmask = np.kron(np.eye(2, dtype=np.float32), np.ones((128, 128), np.float32))
    return jnp.asarray(tall, BF16), jnp.asarray(ones_blk, BF16), jnp.asarray(pmask, F32)


def _hg_kernel(q_ref, lfh_ref, lfl_ref, k_ref, i_ref, g_ref, tall_ref, ones_ref, pmask_ref, nw_ref,
               o_ref, st_ref, *, nch):
    @pl.when(pl.program_id(1) == 0)
    def _():
        st_ref[...] = jnp.zeros_like(st_ref)

    row = lax.broadcasted_iota(jnp.int32, (CH, 1), 0)
    trow = lax.broadcasted_iota(jnp.int32, (CH, 2 * CH), 0)
    scol = lax.broadcasted_iota(jnp.int32, (CH, 2 * CH), 1) % CH
    zero128 = jnp.zeros((CH, HG_DK), BF16)
    zero512 = jnp.zeros((CH, GW), BF16)

    def one(s, rows):
        qb = q_ref[s, rows, :]
        kb = k_ref[s, rows, :]
        vb = i_ref[s, rows, :]
        zall = _dot(tall_ref[...], lfh_ref[s, rows, :]) + _dot(tall_ref[...], lfl_ref[s, rows, :])
        nl = len(HG_LEVELS)
        cum = zall[nl * CH:(nl + 1) * CH, :]
        rem = zall[(nl + 1) * CH:(nl + 2) * CH, :]
        acc = [jnp.zeros((CH, 2 * CH), F32), jnp.zeros((CH, 2 * CH), F32)]
        for li, cs in enumerate(HG_LEVELS):
            fh = (row % cs) < (cs // 2)
            x = jnp.where(fh, kb, qb) * jnp.exp(zall[li * CH:(li + 1) * CH, :]).astype(BF16)
            qt = jnp.where(fh, zero512, x)
            kt = jnp.where(fh, x, zero512)
            lmask = (((trow // cs) == (scol // cs)) & ((trow % cs) >= (cs // 2))
                     & ((scol % cs) < (cs // 2))).astype(F32)
            for p in range(2):
                a0 = kt[:, (2 * p) * 128:(2 * p + 1) * 128]
                a1 = kt[:, (2 * p + 1) * 128:(2 * p + 2) * 128]
                kstack = jnp.concatenate([jnp.concatenate([a0, zero128], axis=1),
                                          jnp.concatenate([zero128, a1], axis=1)], axis=0)
                sc = _dot_nt(qt[:, p * 256:(p + 1) * 256], kstack)
                acc[p] = acc[p] + sc * lmask
        r_same = _dot(qb * kb, ones_ref[...])
        qd = qb * jnp.exp(cum).astype(BF16)
        kw = kb * jnp.exp(rem).astype(BF16)
        elast = jnp.exp(cum[CH - 1:CH, :])
        ys = []
        for p in range(2):
            pl_ = slice(p * 256, (p + 1) * 256)
            v0 = vb[:, (2 * p) * 128:(2 * p + 1) * 128]
            v1 = vb[:, (2 * p + 1) * 128:(2 * p + 2) * 128]
            vstack = jnp.concatenate([jnp.concatenate([v0, zero128], axis=1),
                                      jnp.concatenate([zero128, v1], axis=1)], axis=0)
            st = st_ref[s, p]
            yp = _dot(acc[p].astype(BF16), vstack) + _dot_nt(qd[:, pl_], st.astype(BF16))
            st_ref[s, p] = st * elast[:, pl_] + _dot_tn(vb[:, pl_], kw[:, pl_]) * pmask_ref[...]
            ys.append(yp)
        o = jnp.concatenate(ys, axis=1) + r_same * vb.astype(F32)
        outs = [_rms(o[:, h * 128:(h + 1) * 128], nw_ref[:, h * 128:(h + 1) * 128])
                for h in range(HG_HEADS)]
        o_ref[s, rows, :] = jnp.concatenate(outs, axis=1) * g_ref[s, rows, :].astype(F32)

    def chunk(c, carry):
        rows = pl.ds(pl.multiple_of(c * CH, CH), CH)
        for s in range(NBP):
            one(s, rows)
        return carry

    lax.fori_loop(0, nch, chunk, 0)


def _hg_call(proj3, lfl3, hk3, tabs, nw, l, tt):
    tp = proj3.shape[1]
    tall, ones_blk, pmask = tabs
    xspec = pl.BlockSpec((NBP, tt, GW), lambda bb, i: (bb, i, 0))
    return pl.pallas_call(
        functools.partial(_hg_kernel, nch=tt // CH),
        grid=(NB // NBP, tp // tt),
        in_specs=[_mixer_grp_spec(tt, G_HQ), _mixer_grp_spec(tt, G_HF), xspec, xspec,
                  _mixer_grp_spec(tt, G_HI), _mixer_grp_spec(tt, G_HG),
                  _const_spec(tall.shape), _const_spec(ones_blk.shape), _const_spec(pmask.shape),
                  _layer_spec((1, GW), l)],
        out_specs=pl.BlockSpec((NBP, tt, GW), lambda bb, i: (bb, i, 0)),
        out_shape=jax.ShapeDtypeStruct((NB, tp, GW), F32),
        scratch_shapes=[pltpu.VMEM((NBP, 2, 256, 256), F32)],
        compiler_params=_mixer_params(2 * NBP * tt * GW * (6 * 2 + 4) + (4 << 20)),
        name="hgrn2",
    )(proj3, proj3, lfl3, hk3, proj3, proj3, tall, ones_blk, pmask, nw)


def _merge_kernel(h_ref, y0_ref, y1_ref, y2_ref, y3_ref, npre_ref, wg_ref, wb_ref, wo_ref,
                  npost_ref, o_ref, *, tm, tps):
    h = h_ref[...]
    u = _rms(h, npre_ref[...]).astype(BF16)
    mixed = jnp.zeros((tm, D_MODEL), F32)
    for n, y_ref in enumerate((y0_ref, y1_ref, y2_ref, y3_ref)):
        gate = _sigmoid(_dot(u, wg_ref[:, n * D_MODEL:(n + 1) * D_MODEL]))
        mixed = mixed + gate * _dot(y_ref[...].astype(BF16), wb_ref[n])
    out = _dot(mixed.astype(BF16), wo_ref[...])
    valid = _valid_rows(pl.program_id(0), tps, tm)
    o_ref[...] = jnp.where(valid, h + _rms(out, npost_ref[...]), 0.0)


def _merge_call(h, ys, npre, wg, wb, wo, npost, l, tm, tps):
    rows = h.shape[0]
    nbytes = (2 * (2 * tm * D_MODEL * 4 + 4 * tm * GW * 4)
              + (D_MODEL * 4 * D_MODEL + 4 * GW * D_MODEL + D_MODEL * D_MODEL) * 2
              + 6 * tm * D_MODEL * 4)
    yspec = pl.BlockSpec((tm, GW), lambda i: (i, 0))
    return pl.pallas_call(
        functools.partial(_merge_kernel, tm=tm, tps=tps),
        grid=(rows // tm,),
        in_specs=[pl.BlockSpec((tm, D_MODEL), lambda i: (i, 0)), yspec, yspec, yspec, yspec,
                  _layer_spec((1, D_MODEL), l), _layer_spec((D_MODEL, 4 * D_MODEL), l),
                  _layer_spec((N_BRANCH, GW, D_MODEL), l), _layer_spec((D_MODEL, D_MODEL), l),
                  _layer_spec((1, D_MODEL), l)],
        out_specs=pl.BlockSpec((tm, D_MODEL), lambda i: (i, 0)),
        out_shape=jax.ShapeDtypeStruct((rows, D_MODEL), F32),
        compiler_params=pltpu.CompilerParams(
            dimension_semantics=("parallel",), vmem_limit_bytes=_vmem_limit(nbytes)),
        name="merge",
    )(h, *ys, npre, wg, wb, wo, npost)


def _mlp_kernel(h_ref, npre_ref, wu_ref, wd_ref, npost_ref, o_ref, *, tm, tps, fc):
    h = h_ref[...]
    m = _rms(h, npre_ref[...]).astype(BF16)
    acc = jnp.zeros((tm, D_MODEL), F32)
    for c in range(D_FF // fc):
        a = jnp.maximum(_dot(m, wu_ref[:, c * fc:(c + 1) * fc]), 0.0)
        acc = acc + _dot((a * a).astype(BF16), wd_ref[c * fc:(c + 1) * fc, :])
    valid = _valid_rows(pl.program_id(0), tps, tm)
    o_ref[...] = jnp.where(valid, h + _rms(acc, npost_ref[...]), 0.0)


def _mlp_call(h, npre, wu, wd, npost, l, tm, tps):
    rows = h.shape[0]
    nbytes = 2 * (2 * tm * D_MODEL * 4) + (2 * D_MODEL * D_FF * 2) + 6 * tm * D_MODEL * 4
    return pl.pallas_call(
        functools.partial(_mlp_kernel, tm=tm, tps=tps, fc=1024),
        grid=(rows // tm,),
        in_specs=[pl.BlockSpec((tm, D_MODEL), lambda i: (i, 0)), _layer_spec((1, D_MODEL), l),
                  _layer_spec((D_MODEL, D_FF), l), _layer_spec((D_FF, D_MODEL), l),
                  _layer_spec((1, D_MODEL), l)],
        out_specs=pl.BlockSpec((tm, D_MODEL), lambda i: (i, 0)),
        out_shape=jax.ShapeDtypeStruct((rows, D_MODEL), F32),
        compiler_params=pltpu.CompilerParams(
            dimension_semantics=("parallel",), vmem_limit_bytes=_vmem_limit(nbytes)),
        name="mlp",
    )(h, npre, wu, wd, npost)


def _prep_w_in(w_in):
    depth = w_in.shape[0]
    perm = np.zeros(256, np.int64)
    for half in range(2):
        for h in range(RET_HEADS):
            for i in range(32):
                perm[half * 128 + h * 32 + i] = h * 64 + half * 32 + i
    rq = w_in[:, :, 0:256][:, :, perm]
    rk = w_in[:, :, 256:512][:, :, perm]
    rest = w_in[:, :, 512:3584]
    hg = w_in[:, :, 3592:5640]
    dt = jnp.pad(w_in[:, :, 3584:3592], ((0, 0), (0, 0), (0, GW - SSD_HEADS)))
    wa = jnp.concatenate([rq, rk, rest, hg, dt], axis=2).astype(BF16)
    wg = w_in[:, :, 5640:].astype(BF16)
    assert wa.shape == (depth, D_MODEL, NA)
    return wa, wg


def _prep_s5(lam_re, lam_im, b_re, b_im, c_re, c_im, log_step):
    depth = lam_re.shape[0]
    step = jnp.exp(log_step.astype(F32))[..., None]
    lr, li = lam_re.astype(F32), lam_im.astype(F32)
    mag = jnp.exp(lr * step)
    ab_re, ab_im = mag * jnp.cos(li * step), mag * jnp.sin(li * step)
    inv = 1.0 / (lr * lr + li * li)
    co_re = ((ab_re - 1.0) * lr + ab_im * li) * inv
    co_im = (ab_im * lr - (ab_re - 1.0) * li) * inv
    br, bi = b_re.astype(F32), b_im.astype(F32)
    bb_re = co_re[..., None] * br - co_im[..., None] * bi
    bb_im = co_re[..., None] * bi + co_im[..., None] * br
    gps = S5_GROUPS // S5_SG
    n_tiles = 2 * S5_P2 // MXU_T
    gpt = MXU_T // S5_STATE
    bb = jnp.stack([bb_re, bb_im], axis=1).reshape(depth, 2, S5_SG, 2, gpt, S5_STATE, S5_GROUP)
    bb = bb.transpose(0, 2, 1, 3, 4, 6, 5)
    slab = bb[..., None, :] * jnp.eye(gpt, dtype=F32)[:, None, :, None]
    slab = slab.reshape(depth, n_tiles, gpt * S5_GROUP, MXU_T)
    nt = np.arange(n_tiles)
    at = np.zeros((n_tiles, MXU_T // (gpt * S5_GROUP)), np.float32)
    at[nt, 2 * ((nt // 4) % 2) + nt % 2] = 1.0
    bm = slab[:, :, None, :, :] * jnp.asarray(at)[None, :, :, None, None]
    bm = bm.reshape(depth, n_tiles, MXU_T, MXU_T).astype(BF16)
    cc = jnp.stack([c_re.astype(F32), c_im.astype(F32) * (-1.0)], axis=1)
    cc = cc.reshape(depth, 2, S5_SG, gps, S5_GROUP, S5_STATE).transpose(0, 2, 1, 3, 5, 4)
    cblk = cc[..., None, :] * jnp.eye(gps, dtype=F32)[:, None, :, None]
    cblk = cblk.reshape(depth, S5_SG, S5_SGW, gps * S5_GROUP)
    ac = np.zeros((S5_SG, MXU_T // (gps * S5_GROUP)), np.float32)
    ac[np.arange(S5_SG), np.arange(S5_SG) % 2] = 1.0
    cm = cblk[:, :, :, None, :] * jnp.asarray(ac)[None, :, None, :, None]
    cm = cm.reshape(depth, S5_SG, S5_SGW, MXU_T).astype(BF16)
    ar = jnp.broadcast_to(ab_re.reshape(depth, 1, S5_P2), (depth, NB, S5_P2))
    ai = jnp.broadcast_to(ab_im.reshape(depth, 1, S5_P2), (depth, NB, S5_P2))
    return bm, ar, ai, cm


def _rope_tables(tp):
    half = RET_DK // 2
    inv_freq = RET_ROPE_BASE ** (-jnp.arange(half, dtype=F32) / half)
    pos = jnp.maximum(jnp.arange(tp, dtype=F32) - PADF, 0.0)
    ang = pos[:, None] * inv_freq[None, :]
    return jnp.tile(jnp.cos(ang), (1, RET_HEADS)), jnp.tile(jnp.sin(ang), (1, RET_HEADS))


def _pick_tiles(tp):
    tt = CH
    tm = next(c for c in (544, 256, CH) if tp % c == 0)
    return tt, tm


def kernel(x, meta_tokens, w_in, w_branch, w_out, norm_pre_mix, norm_post_mix, norm_pre_mlp, norm_post_mlp, w_up, w_down, ret_gn_w, s5_lam_re, s5_lam_im, s5_b_re, s5_b_im, s5_c_re, s5_c_im, s5_d, s5_log_step, s5_glu_w, s5_glu_b, ssd_conv_w, ssd_conv_b, ssd_dt_bias, ssd_a_log, ssd_d, ssd_norm_w, hgrn_lb, hgrn_norm_w):
    bsz, seq, _ = x.shape
    depth = w_in.shape[0]
    assert bsz == NB and x.shape[2] == D_MODEL
    tp = PADF + N_META + seq
    assert tp % CH == 0
    tt, tm = _pick_tiles(tp)
    rows = tp * NB

    h = _assemble(x, meta_tokens)

    wa, wg = _prep_w_in(w_in)
    wb = w_branch.astype(BF16)
    wo = w_out.astype(BF16)
    wu = w_up.astype(BF16)
    wd = w_down.astype(BF16)
    r3 = lambda a: a.astype(F32).reshape(depth, 1, -1)
    npre, npost, mpre, mpost = r3(norm_pre_mix), r3(norm_post_mix), r3(norm_pre_mlp), r3(norm_post_mlp)
    bm, ar, ai, cm = _prep_s5(s5_lam_re, s5_lam_im, s5_b_re, s5_b_im, s5_c_re, s5_c_im, s5_log_step)
    s5gw = s5_glu_w.astype(BF16)
    s5gb = r3(s5_glu_b)
    s5d = r3(s5_d)
    zc = jnp.zeros((depth, SSD_CONV, GW), F32)
    cw = jnp.concatenate([zc] * G_XS + [ssd_conv_w[:, :, :GW].astype(F32), ssd_conv_w[:, :, GW:].astype(F32)]
                         + [zc] * (N_GROUPS - G_BC - 1), axis=2)
    zb = jnp.zeros((depth, 1, GW), F32)
    cb = jnp.concatenate([zb] * G_XS + [r3(ssd_conv_b[:, :GW]), r3(ssd_conv_b[:, GW:])]
                         + [zb] * (N_GROUPS - G_BC - 1), axis=2)
    pad8 = lambda a: jnp.pad(a.astype(F32), ((0, 0), (0, 128 - SSD_HEADS))).reshape(depth, 1, 128)
    dtb = pad8(ssd_dt_bias)
    ssd_a = pad8(-jnp.exp(ssd_a_log.astype(F32)))
    ssd_dsk = r3(jnp.repeat(ssd_d.astype(F32), SSD_HEADDIM, axis=1))
    ssd_nw = r3(ssd_norm_w)
    lb_all = jnp.cumsum(jax.nn.softmax(hgrn_lb.astype(F32), axis=0), axis=0)
    lb_all = r3(lb_all - lb_all[0])
    hg_nw = r3(hgrn_norm_w)
    gnw = r3(ret_gn_w)

    cos_t, sin_t = _rope_tables(tp)
    ret_tabs, ssd_tabs, hg_tabs = _ret_tables(), _ssd_tables(), _hg_tables()

    flat = lambda y: y.reshape(rows, GW)
    v3 = lambda y: y.reshape(NB, tp, y.shape[-1])
    for l in range(depth):
        proj, lfl, hk = _inproj(h, npre, wa, cos_t, sin_t, cw, cb, lb_all, dtb, ssd_a, l, tp)
        proj3 = v3(proj)
        y_ret = _ret_call(proj3, ret_tabs, gnw, l, tt)
        y_s5 = _s5_call(proj3, bm, ar, ai, cm, s5d, s5gw, s5gb, l, S5_TT)
        y_ssd = _ssd_call(proj3, ssd_tabs, ssd_dsk, ssd_nw, l, tt)
        y_hg = _hg_call(proj3, v3(lfl), v3(hk), hg_tabs, hg_nw, l, tt)
        h = _merge_call(h, (flat(y_ret), flat(y_s5), flat(y_ssd), flat(y_hg)),
                        npre, wg, wb, wo, npost, l, tm, tp // tm)
        h = _mlp_call(h, mpre, wu, wd, mpost, l, tm, tp // tm)

    return h.reshape(NB, tp, D_MODEL)[:, PADF + N_META:].astype(x.dtype)
```

```python
import functools

import numpy as np
import jax
import jax.numpy as jnp
from jax import lax
from jax.experimental import pallas as pl
from jax.experimental.pallas import tpu as pltpu

F32 = jnp.float32
BF16 = jnp.bfloat16

D_MODEL = 1024
N_META = 16
BRANCH_WIDTH = 512
N_BRANCH = 4
D_FF = 4 * D_MODEL
NORM_EPS = 1e-6
RET_HEADS = 4
RET_DK = 64
RET_DV = 128
RET_ROPE_BASE = 10000.0
S5_GROUP = 16
S5_GROUPS = 32
S5_STATE = 64
SSD_HEADDIM = 64
SSD_HEADS = 8
SSD_GROUPS = 2
SSD_STATE = 128
SSD_CONV = 4
HG_HEADS = 4
HG_DK = 128

NB = 8
PADF = 112
CH = 128
S5_TT = 64
GW = 512
N_GROUPS = 12
NA = GW * N_GROUPS
G_QK, G_RV, G_RG, G_S5, G_Z, G_XS, G_BC, G_HQ, G_HF, G_HI, G_HG, G_DT = range(N_GROUPS)
S5_P2 = S5_GROUPS * S5_STATE
S5_SG = 4
S5_SGW = 2 * S5_P2 // S5_SG
MXU_T = 256
NBP = 8
VMEM_LIMIT_CAP = 56 * 1024 * 1024


def _vmem_limit(nbytes):
    return int(min(VMEM_LIMIT_CAP, max(32 * 1024 * 1024, nbytes * 5 // 4 + 8 * 1024 * 1024)))


def _const_spec(shape):
    nd = len(shape)
    return pl.BlockSpec(shape, lambda *a: (0,) * nd, pipeline_mode=pl.Buffered(1))


def _layer_spec(shape, l):
    nd = len(shape)
    return pl.BlockSpec((None,) + tuple(shape), lambda *a: (l,) + (0,) * nd,
                        pipeline_mode=pl.Buffered(1))


def _split_bf16(x):
    hi = x.astype(BF16)
    lo = (x - hi.astype(F32)).astype(BF16)
    return hi, lo


def _dot(a, b):
    return jnp.dot(a, b, preferred_element_type=F32)


def _dot_nt(a, b):
    return lax.dot_general(a, b, (((1,), (1,)), ((), ())), preferred_element_type=F32)


def _dot_tn(a, b):
    return lax.dot_general(a, b, (((0,), (0,)), ((), ())), preferred_element_type=F32)


def _dot_sel(x, e_bf16):
    hi, lo = _split_bf16(x)
    return _dot(hi, e_bf16) + _dot(lo, e_bf16)


def _sigmoid(x):
    return 1.0 / (1.0 + jnp.exp(-x))


def _silu(x):
    return x * _sigmoid(x)


def _rms(x, w):
    ms = jnp.mean(x * x, axis=-1, keepdims=True)
    return x * lax.rsqrt(ms + NORM_EPS) * w


def _valid_rows(tile_idx, tiles_per_seq, tm):
    tpos = (tile_idx % tiles_per_seq) * tm + lax.broadcasted_iota(jnp.int32, (tm, 1), 0)
    return tpos >= PADF


def _assemble_kernel(x_ref, meta_ref, o_ref):
    o_ref[0:PADF, :] = jnp.zeros((PADF, D_MODEL), F32)
    o_ref[PADF:PADF + N_META, :] = meta_ref[...].astype(F32)
    o_ref[PADF + N_META:, :] = x_ref[...].astype(F32)


def _assemble(x, meta_tokens):
    bsz, seq, _ = x.shape
    tp = PADF + N_META + seq
    return pl.pallas_call(
        _assemble_kernel,
        grid=(bsz,),
        in_specs=[pl.BlockSpec((seq, D_MODEL), lambda b: (b, 0)), _const_spec((N_META, D_MODEL))],
        out_specs=pl.BlockSpec((tp, D_MODEL), lambda b: (b, 0)),
        out_shape=jax.ShapeDtypeStruct((bsz * tp, D_MODEL), F32),
        compiler_params=pltpu.CompilerParams(
            dimension_semantics=("parallel",),
            vmem_limit_bytes=_vmem_limit(2 * (seq + tp) * D_MODEL * 4)),
        name="assemble",
    )(x.reshape(bsz * seq, D_MODEL), meta_tokens)


def _inproj_kernel(h_ref, nw_ref, w_ref, cos_ref, sin_ref, cw_ref, cb_ref, lb_ref, dtb_ref, a_ref,
                   o_ref, lfl_ref, hk_ref, xn_ref, *, tm):
    j = pl.program_id(1)

    @pl.when(j == 0)
    def _():
        xn_ref[...] = _rms(h_ref[...], nw_ref[...]).astype(BF16)

    def mm():
        return _dot(xn_ref[...], w_ref[...])

    @pl.when((j == G_RV) | (j == G_S5) | (j == G_HI))
    def _():
        o_ref[...] = mm().astype(BF16)

    @pl.when((j == G_RG) | (j == G_Z) | (j == G_HQ) | (j == G_HG))
    def _():
        o_ref[...] = _silu(mm()).astype(BF16)

    @pl.when((j == G_XS) | (j == G_BC))
    def _():
        r = mm()
        acc = cb_ref[...] + r * cw_ref[SSD_CONV - 1:SSD_CONV, :]
        for back in range(1, SSD_CONV):
            tap = SSD_CONV - 1 - back
            acc = acc + pltpu.roll(r, back, 0) * cw_ref[tap:tap + 1, :]
        valid = _valid_rows(0, 1, tm)
        o_ref[...] = jnp.where(valid, _silu(acc), 0.0).astype(BF16)

    @pl.when(j == G_QK)
    def _():
        r = mm()
        cs = cos_ref[...]
        sn = sin_ref[...]
        q1, q2, k1, k2 = r[:, 0:128], r[:, 128:256], r[:, 256:384], r[:, 384:512]
        o_ref[:, 0:128] = (q1 * cs - q2 * sn).astype(BF16)
        o_ref[:, 128:256] = (q1 * sn + q2 * cs).astype(BF16)
        o_ref[:, 256:384] = ((k1 * cs - k2 * sn) * (RET_DK ** -0.5)).astype(BF16)
        o_ref[:, 384:512] = ((k1 * sn + k2 * cs) * (RET_DK ** -0.5)).astype(BF16)

    @pl.when(j == G_HF)
    def _():
        lb = lb_ref[...]
        f = lb + (1.0 - lb) * _sigmoid(mm())
        hi, lo = _split_bf16(jnp.log(f))
        o_ref[...] = hi
        lfl_ref[...] = lo
        hk_ref[...] = (1.0 - f).astype(BF16)

    @pl.when(j == G_DT)
    def _():
        dt = jax.nn.softplus(mm()[:, 0:128] + dtb_ref[...])
        dhi, dlo = _split_bf16(dt)
        lhi, llo = _split_bf16(dt * a_ref[...])
        o_ref[:, 0:128] = dhi
        o_ref[:, 128:256] = dlo
        o_ref[:, 256:384] = lhi
        o_ref[:, 384:512] = llo


def _inproj(h, nw, w, cos_t, sin_t, cw, cb, lb, dtb, a, l, tm):
    rows = h.shape[0]
    assert cos_t.shape[0] == tm
    nbytes = (2 * tm * D_MODEL * 4 + tm * D_MODEL * 2 + 2 * D_MODEL * GW * 2 + 3 * 2 * tm * GW * 2
              + 2 * tm * 128 * 4 + 4 * tm * GW * 4)
    ospec0 = pl.BlockSpec((tm, GW), lambda i, j: (i, 0))
    return pl.pallas_call(
        functools.partial(_inproj_kernel, tm=tm),
        grid=(rows // tm, N_GROUPS),
        in_specs=[
            pl.BlockSpec((tm, D_MODEL), lambda i, j: (i, 0)),
            pl.BlockSpec((None, 1, D_MODEL), lambda i, j: (l, 0, 0)),
            pl.BlockSpec((None, D_MODEL, GW), lambda i, j: (l, 0, j)),
            _const_spec((tm, 128)), _const_spec((tm, 128)),
            pl.BlockSpec((None, SSD_CONV, GW), lambda i, j: (l, 0, j)),
            pl.BlockSpec((None, 1, GW), lambda i, j: (l, 0, j)),
            _layer_spec((1, GW), l), _layer_spec((1, 128), l), _layer_spec((1, 128), l),
        ],
        out_specs=[pl.BlockSpec((tm, GW), lambda i, j: (i, j)), ospec0, ospec0],
        out_shape=[jax.ShapeDtypeStruct((rows, NA), BF16), jax.ShapeDtypeStruct((rows, GW), BF16),
                   jax.ShapeDtypeStruct((rows, GW), BF16)],
        scratch_shapes=[pltpu.VMEM((tm, D_MODEL), BF16)],
        compiler_params=pltpu.CompilerParams(
            dimension_semantics=("parallel", "arbitrary"), vmem_limit_bytes=_vmem_limit(nbytes)),
        name="inproj",
    )(h, nw, w, cos_t, sin_t, cw, cb, lb, dtb, a)


def _mixer_grp_spec(tt, g):
    return pl.BlockSpec((NBP, tt, GW), lambda bb, i, g=g: (bb, i, g))


def _mixer_params(est_bytes):
    return pltpu.CompilerParams(dimension_semantics=("parallel", "arbitrary"),
                                vmem_limit_bytes=_vmem_limit(est_bytes))


def _ret_tables():
    gam = 1.0 - np.exp2(-5.0 - np.arange(RET_HEADS, dtype=np.float64))
    lg = np.log(gam)
    t = np.arange(CH)
    col_head = (np.arange(2 * 128) % 128) // 32
    dall = np.zeros((CH, RET_HEADS * CH), np.float64)
    for h in range(RET_HEADS):
        d = np.exp(lg[h] * (t[:, None] - t[None, :]))
        dall[:, h * CH:(h + 1) * CH] = np.where(t[:, None] >= t[None, :], d, 0.0)
    qdec = np.exp(lg[col_head][None, :] * (t[:, None] + 1.0))
    kdec = np.exp(lg[col_head][None, :] * (CH - 1.0 - t[:, None]))
    vcol_head = np.arange(RET_HEADS * RET_DV) // RET_DV
    sdec = np.exp(lg[col_head] * CH)[:, None] * np.ones((1, 512))
    smask = (col_head[:, None] == vcol_head[None, :]).astype(np.float64)
    kmask = (col_head[None, :] == np.arange(RET_HEADS)[:, None]).astype(np.float64)[:, None, :]
    vmask = (vcol_head[None, :] == np.arange(RET_HEADS)[:, None]).astype(np.float64)[:, None, :]
    f = lambda a: jnp.asarray(a, F32)
    b = lambda a: jnp.asarray(a, BF16)
    return f(dall), f(qdec), f(kdec), f(sdec), f(smask), b(kmask), b(vmask)


def _ret_kernel(qk_ref, v_ref, g_ref, dall_ref, qdec_ref, kdec_ref, sdec_ref,
                smask_ref, kmask_ref, vmask_ref, gnw_ref, o_ref, st_ref, *, nch):
    @pl.when(pl.program_id(1) == 0)
    def _():
        st_ref[...] = jnp.zeros_like(st_ref)

    def one(s, rows):
        qk = qk_ref[s, rows, :]
        qb = qk[:, 0:256]
        kb = qk[:, 256:512]
        vb = v_ref[s, rows, :]
        kstack = jnp.concatenate([kb * kmask_ref[h] for h in range(RET_HEADS)], axis=0)
        s_all = _dot_nt(qb, kstack)
        p = (s_all * dall_ref[...]).astype(BF16)
        vstack = jnp.concatenate([vb * vmask_ref[h] for h in range(RET_HEADS)], axis=0)
        y = _dot(p, vstack)
        st = st_ref[s]
        y = y + _dot((qb.astype(F32) * qdec_ref[...]).astype(BF16), st.astype(BF16))
        new = _dot_tn((kb.astype(F32) * kdec_ref[...]).astype(BF16), vb)
        st_ref[s] = st * sdec_ref[...] + new * smask_ref[...]
        outs = []
        for h in range(RET_HEADS):
            yh = y[:, h * RET_DV:(h + 1) * RET_DV]
            mu = jnp.mean(yh, axis=-1, keepdims=True)
            d = yh - mu
            var = jnp.mean(d * d, axis=-1, keepdims=True)
            outs.append(d * lax.rsqrt(var + NORM_EPS))
        yn = jnp.concatenate(outs, axis=1) * gnw_ref[...]
        o_ref[s, rows, :] = g_ref[s, rows, :].astype(F32) * yn

    def chunk(c, carry):
        rows = pl.ds(pl.multiple_of(c * CH, CH), CH)
        for s in range(NBP):
            one(s, rows)
        return carry

    lax.fori_loop(0, nch, chunk, 0)


def _ret_call(proj3, tabs, gnw, l, tt):
    tp = proj3.shape[1]
    dall, qdec, kdec, sdec, smask, kmask, vmask = tabs
    return pl.pallas_call(
        functools.partial(_ret_kernel, nch=tt // CH),
        grid=(NB // NBP, tp // tt),
        in_specs=[_mixer_grp_spec(tt, G_QK), _mixer_grp_spec(tt, G_RV), _mixer_grp_spec(tt, G_RG),
                  _const_spec(dall.shape), _const_spec(qdec.shape), _const_spec(kdec.shape),
                  _const_spec(sdec.shape), _const_spec(smask.shape), _const_spec(kmask.shape),
                  _const_spec(vmask.shape), _layer_spec((1, GW), l)],
        out_specs=pl.BlockSpec((NBP, tt, GW), lambda bb, i: (bb, i, 0)),
        out_shape=jax.ShapeDtypeStruct((NB, tp, GW), F32),
        scratch_shapes=[pltpu.VMEM((NBP, 256, 512), F32)],
        compiler_params=_mixer_params(2 * NBP * tt * GW * (3 * 2 + 4) + (4 << 20)),
        name="retention",
    )(proj3, proj3, proj3, dall, qdec, kdec, sdec, smask, kmask, vmask, gnw)


def _s5_kernel(u_ref, bm_ref, ar_ref, ai_ref, cm_ref, d_ref, gw_ref, gb_ref, o_ref,
               utm_ref, ytm_ref, bu_ref, xs_ref, *, tt):
    @pl.when(pl.program_id(0) == 0)
    def _():
        xs_ref[...] = jnp.zeros_like(xs_ref)

    for b in range(NB):
        utm_ref[:, b, :] = u_ref[b].astype(F32)
    u = utm_ref[...].reshape(tt * NB, GW)
    ub = u.astype(BF16)
    half = S5_SGW // 2
    for nt in range(2 * S5_P2 // MXU_T):
        kb = nt // (2 * S5_P2 // MXU_T // (GW // MXU_T))
        bu_ref[:, nt * MXU_T:(nt + 1) * MXU_T] = _dot(ub[:, kb * MXU_T:(kb + 1) * MXU_T], bm_ref[nt])
    for c in range(S5_SG):
        cr = pl.ds(c * S5_SGW, half)
        ci = pl.ds(c * S5_SGW + half, half)
        ar = ar_ref[:, pl.ds(c * half, half)]
        ai = ai_ref[:, pl.ds(c * half, half)]

        def step(t, carry):
            xr, xi = carry
            rows = pl.ds(pl.multiple_of(t * NB, NB), NB)
            nr = ar * xr - ai * xi + bu_ref[rows, cr]
            ni = ar * xi + ai * xr + bu_ref[rows, ci]
            bu_ref[rows, cr] = nr
            bu_ref[rows, ci] = ni
            return nr, ni

        xr, xi = lax.fori_loop(0, tt, step, (xs_ref[:, cr], xs_ref[:, ci]), unroll=True)
        xs_ref[:, cr] = xr
        xs_ref[:, ci] = xi
    ys = []
    for j in range(GW // MXU_T):
        acc = None
        for c in range(j * (S5_SG // 2), (j + 1) * (S5_SG // 2)):
            part = _dot(bu_ref[:, c * S5_SGW:(c + 1) * S5_SGW].astype(BF16), cm_ref[c])
            acc = part if acc is None else acc + part
        ys.append(acc)
    y = jnp.concatenate(ys, axis=1) + d_ref[...] * u
    y = jax.nn.gelu(y)
    ag = _dot(y.astype(BF16), gw_ref[...]) + gb_ref[...]
    ytm_ref[...] = (ag[:, :BRANCH_WIDTH] * _sigmoid(ag[:, BRANCH_WIDTH:])).reshape(tt, NB, GW)
    for b in range(NB):
        o_ref[b] = ytm_ref[:, b, :]


def _s5_call(proj3, bm, ar, ai, cm, dsk, gw, gb, l, tt):
    tp = proj3.shape[1]
    rb = tt * NB
    nbytes = (2 * rb * GW * 2 + 2 * rb * GW * 4 + 2 * rb * GW * 4 + rb * 2 * S5_P2 * 4 * 2
              + (bm.shape[1] * MXU_T * MXU_T + cm.shape[1] * S5_SGW * MXU_T + GW * 2 * GW) * 2)
    return pl.pallas_call(
        functools.partial(_s5_kernel, tt=tt),
        grid=(tp // tt,),
        in_specs=[pl.BlockSpec((NB, tt, GW), lambda i: (0, i, G_S5)),
                  _layer_spec(bm.shape[1:], l), _layer_spec((NB, S5_P2), l),
                  _layer_spec((NB, S5_P2), l), _layer_spec(cm.shape[1:], l),
                  _layer_spec((1, GW), l), _layer_spec((GW, 2 * GW), l), _layer_spec((1, 2 * GW), l)],
        out_specs=pl.BlockSpec((NB, tt, GW), lambda i: (0, i, 0)),
        out_shape=jax.ShapeDtypeStruct((NB, tp, GW), F32),
        scratch_shapes=[pltpu.VMEM((tt, NB, GW), F32), pltpu.VMEM((tt, NB, GW), F32),
                        pltpu.VMEM((rb, 2 * S5_P2), F32), pltpu.VMEM((NB, 2 * S5_P2), F32)],
        compiler_params=pltpu.CompilerParams(
            dimension_semantics=("arbitrary",), vmem_limit_bytes=_vmem_limit(nbytes)),
        name="s5",
    )(proj3, bm, ar, ai, cm, dsk, gw, gb)


def _ssd_tables():
    t = np.arange(CH)
    ltri = (t[:, None] >= t[None, :]).astype(np.float32)
    expand = np.zeros((128, GW), np.float32)
    for h in range(SSD_HEADS):
        expand[h, h * SSD_HEADDIM:(h + 1) * SSD_HEADDIM] = 1.0
    hm = np.zeros((4, 1, 256), np.float32)
    for h in range(4):
        hm[h, 0, h * SSD_HEADDIM:(h + 1) * SSD_HEADDIM] = 1.0
    return jnp.asarray(ltri, BF16), jnp.asarray(expand, BF16), jnp.asarray(hm, BF16)


def _ssd_kernel(z_ref, xs_ref, bc_ref, dt_ref, ltri_ref, exp_ref, hm_ref, dsk_ref, nw_ref, o_ref,
                st_ref, *, nch):
    @pl.when(pl.program_id(1) == 0)
    def _():
        st_ref[...] = jnp.zeros_like(st_ref)

    tri = lax.broadcasted_iota(jnp.int32, (CH, CH), 0) >= lax.broadcasted_iota(jnp.int32, (CH, CH), 1)
    hpg = SSD_HEADS // SSD_GROUPS

    def one(s, rows):
        xs = xs_ref[s, rows, :].astype(F32)
        bc = bc_ref[s, rows, :]
        dtg = dt_ref[s, rows, :]
        cum = _dot(ltri_ref[...], dtg[:, 256:384]) + _dot(ltri_ref[...], dtg[:, 384:512])
        dt_e = _dot(dtg[:, 0:128], exp_ref[...]) + _dot(dtg[:, 128:256], exp_ref[...])
        cum_e = _dot_sel(cum, exp_ref[...])
        last_e = cum_e[CH - 1:CH, :]
        v = xs * dt_e
        vb = v.astype(BF16)
        wv = (v * jnp.exp(last_e - cum_e)).astype(BF16)
        ecum = jnp.exp(cum_e)
        elast = jnp.exp(last_e)
        cum_t = cum.T
        ys = []
        for g in range(SSD_GROUPS):
            gl = slice(g * 256, (g + 1) * 256)
            bm = bc[:, g * 128:(g + 1) * 128]
            cm = bc[:, 256 + g * 128:256 + (g + 1) * 128]
            scores = _dot_nt(cm, bm)
            ps = []
            for hh in range(hpg):
                h = g * hpg + hh
                seg = cum[:, h:h + 1] - cum_t[h:h + 1, :]
                ps.append(scores * jnp.exp(jnp.where(tri, seg, -jnp.inf)))
            pcat = jnp.concatenate(ps, axis=1).astype(BF16)
            vg = vb[:, gl]
            vstack = jnp.concatenate([vg * hm_ref[hh] for hh in range(hpg)], axis=0)
            st = st_ref[s, g]
            yg = _dot(pcat, vstack) + _dot(cm, st.astype(BF16)) * ecum[:, gl]
            st_ref[s, g] = st * elast[:, gl] + _dot_tn(bm, wv[:, gl])
            ys.append(yg)
        y = jnp.concatenate(ys, axis=1) + dsk_ref[...] * xs
        y = y * z_ref[s, rows, :].astype(F32)
        outs = [_rms(y[:, g * 256:(g + 1) * 256], nw_ref[:, g * 256:(g + 1) * 256])
                for g in range(SSD_GROUPS)]
        o_ref[s, rows, :] = jnp.concatenate(outs, axis=1)

    def chunk(c, carry):
        rows = pl.ds(pl.multiple_of(c * CH, CH), CH)
        for s in range(NBP):
            one(s, rows)
        return carry

    lax.fori_loop(0, nch, chunk, 0)


def _ssd_call(proj3, tabs, dsk, nw, l, tt):
    tp = proj3.shape[1]
    ltri, expand, hm = tabs
    return pl.pallas_call(
        functools.partial(_ssd_kernel, nch=tt // CH),
        grid=(NB // NBP, tp // tt),
        in_specs=[_mixer_grp_spec(tt, G_Z), _mixer_grp_spec(tt, G_XS), _mixer_grp_spec(tt, G_BC),
                  _mixer_grp_spec(tt, G_DT),
                  _const_spec(ltri.shape), _const_spec(expand.shape), _const_spec(hm.shape),
                  _layer_spec((1, GW), l), _layer_spec((1, GW), l)],
        out_specs=pl.BlockSpec((NBP, tt, GW), lambda bb, i: (bb, i, 0)),
        out_shape=jax.ShapeDtypeStruct((NB, tp, GW), F32),
        scratch_shapes=[pltpu.VMEM((NBP, SSD_GROUPS, SSD_STATE, 256), F32)],
        compiler_params=_mixer_params(2 * NBP * tt * GW * (4 * 2 + 4) + (4 << 20)),
        name="ssd",
    )(proj3, proj3, proj3, proj3, ltri, expand, hm, dsk, nw)


HG_LEVELS = tuple(CH >> i for i in range(CH.bit_length() - 1))
HG_MM_LEVELS = tuple(c for c in HG_LEVELS if c < 16)


def _hg_tables():
    t = np.arange(CH)
    blocks = []
    for c in HG_MM_LEVELS:
        m = np.zeros((CH, CH), np.float32)
        for r in range(CH):
            start = (r // c) * c
            mid = start + c // 2 - 1
            if r <= mid:
                m[r, r + 1:mid + 1] = 1.0
            else:
                m[r, mid + 1:r + 1] = 1.0
        blocks.append(m)
    blocks.append((t[:, None] >= t[None, :]).astype(np.float32))
    tall = np.concatenate(blocks, axis=0)
    ones_blk = np.kron(np.eye(HG_HEADS, dtype=np.float32), np.ones((HG_DK, HG_DK), np.float32))
    pmask = np.kron(np.eye(2, dtype=np.float32), np.ones((128, 128), np.float32))
    s2 = np.arange(2 * CH) % CH
    lm = np.stack([((t[:, None] // c == s2[None, :] // c) & (t[:, None] % c >= c // 2)
                    & (s2[None, :] % c < c // 2)).astype(np.float32) for c in HG_LEVELS])
    return (jnp.asarray(tall, BF16), jnp.asarray(ones_blk, BF16), jnp.asarray(pmask, F32),
            jnp.asarray(lm, F32))


def _hg_kernel(q_ref, lfh_ref, lfl_ref, k_ref, i_ref, g_ref, tall_ref, ones_ref, pmask_ref, lm_ref,
               nw_ref, o_ref, st_ref, *, nch):
    @pl.when(pl.program_id(1) == 0)
    def _():
        st_ref[...] = jnp.zeros_like(st_ref)

    row = lax.broadcasted_iota(jnp.int32, (CH, 1), 0)
    zero128 = jnp.zeros((CH, HG_DK), BF16)
    zero512 = jnp.zeros((CH, GW), BF16)

    def one(s, rows):
        qb = q_ref[s, rows, :]
        kb = k_ref[s, rows, :]
        vb = i_ref[s, rows, :]
        zall = _dot(tall_ref[...], lfh_ref[s, rows, :]) + _dot(tall_ref[...], lfl_ref[s, rows, :])
        nmm = len(HG_MM_LEVELS)
        cum = zall[nmm * CH:(nmm + 1) * CH, :]
        rem = cum[CH - 1:CH, :] - cum
        acc = [jnp.zeros((CH, 2 * CH), F32), jnp.zeros((CH, 2 * CH), F32)]
        for cs in HG_LEVELS:
            fh = (row % cs) < (cs // 2)
            if cs in HG_MM_LEVELS:
                li = HG_MM_LEVELS.index(cs)
                zl = zall[li * CH:(li + 1) * CH, :]
            else:
                c3 = cum.reshape(CH // cs, cs, GW)
                mid = c3[:, cs // 2 - 1:cs // 2, :]
                zl = jnp.concatenate([mid - c3[:, :cs // 2, :], c3[:, cs // 2:, :] - mid],
                                     axis=1).reshape(CH, GW)
            x = jnp.where(fh, kb, qb) * jnp.exp(zl).astype(BF16)
            qt = jnp.where(fh, zero512, x)
            kt = jnp.where(fh, x, zero512)
            lmask = lm_ref[HG_LEVELS.index(cs)]
            for p in range(2):
                a0 = kt[:, (2 * p) * 128:(2 * p + 1) * 128]
                a1 = kt[:, (2 * p + 1) * 128:(2 * p + 2) * 128]
                kstack = jnp.concatenate([jnp.concatenate([a0, zero128], axis=1),
                                          jnp.concatenate([zero128, a1], axis=1)], axis=0)
                sc = _dot_nt(qt[:, p * 256:(p + 1) * 256], kstack)
                acc[p] = acc[p] + sc * lmask
        r_same = _dot(qb * kb, ones_ref[...])
        qd = qb * jnp.exp(cum).astype(BF16)
        kw = kb * jnp.exp(rem).astype(BF16)
        elast = jnp.exp(cum[CH - 1:CH, :])
        ys = []
        for p in range(2):
            pl_ = slice(p * 256, (p + 1) * 256)
            v0 = vb[:, (2 * p) * 128:(2 * p + 1) * 128]
            v1 = vb[:, (2 * p + 1) * 128:(2 * p + 2) * 128]
            vstack = jnp.concatenate([jnp.concatenate([v0, zero128], axis=1),
                                      jnp.concatenate([zero128, v1], axis=1)], axis=0)
            st = st_ref[s, p]
            yp = _dot(acc[p].astype(BF16), vstack) + _dot_nt(qd[:, pl_], st.astype(BF16))
            st_ref[s, p] = st * elast[:, pl_] + _dot_tn(vb[:, pl_], kw[:, pl_]) * pmask_ref[...]
            ys.append(yp)
        o = jnp.concatenate(ys, axis=1) + r_same * vb.astype(F32)
        outs = [_rms(o[:, h * 128:(h + 1) * 128], nw_ref[:, h * 128:(h + 1) * 128])
                for h in range(HG_HEADS)]
        o_ref[s, rows, :] = jnp.concatenate(outs, axis=1) * g_ref[s, rows, :].astype(F32)

    def chunk(c, carry):
        rows = pl.ds(pl.multiple_of(c * CH, CH), CH)
        for s in range(NBP):
            one(s, rows)
        return carry

    lax.fori_loop(0, nch, chunk, 0)


def _hg_call(proj3, lfl3, hk3, tabs, nw, l, tt):
    tp = proj3.shape[1]
    tall, ones_blk, pmask, lm = tabs
    xspec = pl.BlockSpec((NBP, tt, GW), lambda bb, i: (bb, i, 0))
    return pl.pallas_call(
        functools.partial(_hg_kernel, nch=tt // CH),
        grid=(NB // NBP, tp // tt),
        in_specs=[_mixer_grp_spec(tt, G_HQ), _mixer_grp_spec(tt, G_HF), xspec, xspec,
                  _mixer_grp_spec(tt, G_HI), _mixer_grp_spec(tt, G_HG),
                  _const_spec(tall.shape), _const_spec(ones_blk.shape), _const_spec(pmask.shape),
                  _const_spec(lm.shape), _layer_spec((1, GW), l)],
        out_specs=pl.BlockSpec((NBP, tt, GW), lambda bb, i: (bb, i, 0)),
        out_shape=jax.ShapeDtypeStruct((NB, tp, GW), F32),
        scratch_shapes=[pltpu.VMEM((NBP, 2, 256, 256), F32)],
        compiler_params=_mixer_params(2 * NBP * tt * GW * (6 * 2 + 4) + (6 << 20)),
        name="hgrn2",
    )(proj3, proj3, lfl3, hk3, proj3, proj3, tall, ones_blk, pmask, lm, nw)


def _merge_kernel(h_ref, y0_ref, y1_ref, y2_ref, y3_ref, npre_ref, wg_ref, wb_ref, wo_ref,
                  npost_ref, o_ref, *, tm, tps):
    h = h_ref[...]
    u = _rms(h, npre_ref[...]).astype(BF16)
    mixed = jnp.zeros((tm, D_MODEL), F32)
    for n, y_ref in enumerate((y0_ref, y1_ref, y2_ref, y3_ref)):
        gate = _sigmoid(_dot(u, wg_ref[:, n * D_MODEL:(n + 1) * D_MODEL]))
        mixed = mixed + gate * _dot(y_ref[...].astype(BF16), wb_ref[n])
    out = _dot(mixed.astype(BF16), wo_ref[...])
    valid = _valid_rows(pl.program_id(0), tps, tm)
    o_ref[...] = jnp.where(valid, h + _rms(out, npost_ref[...]), 0.0)


def _merge_call(h, ys, npre, wg, wb, wo, npost, l, tm, tps):
    rows = h.shape[0]
    nbytes = (2 * (2 * tm * D_MODEL * 4 + 4 * tm * GW * 4)
              + (D_MODEL * 4 * D_MODEL + 4 * GW * D_MODEL + D_MODEL * D_MODEL) * 2
              + 6 * tm * D_MODEL * 4)
    yspec = pl.BlockSpec((tm, GW), lambda i: (i, 0))
    return pl.pallas_call(
        functools.partial(_merge_kernel, tm=tm, tps=tps),
        grid=(rows // tm,),
        in_specs=[pl.BlockSpec((tm, D_MODEL), lambda i: (i, 0)), yspec, yspec, yspec, yspec,
                  _layer_spec((1, D_MODEL), l), _layer_spec((D_MODEL, 4 * D_MODEL), l),
                  _layer_spec((N_BRANCH, GW, D_MODEL), l), _layer_spec((D_MODEL, D_MODEL), l),
                  _layer_spec((1, D_MODEL), l)],
        out_specs=pl.BlockSpec((tm, D_MODEL), lambda i: (i, 0)),
        out_shape=jax.ShapeDtypeStruct((rows, D_MODEL), F32),
        compiler_params=pltpu.CompilerParams(
            dimension_semantics=("parallel",), vmem_limit_bytes=_vmem_limit(nbytes)),
        name="merge",
    )(h, *ys, npre, wg, wb, wo, npost)


def _mlp_kernel(h_ref, npre_ref, wu_ref, wd_ref, npost_ref, o_ref, *, tm, tps, fc):
    h = h_ref[...]
    m = _rms(h, npre_ref[...]).astype(BF16)
    acc = jnp.zeros((tm, D_MODEL), F32)
    for c in range(D_FF // fc):
        a = jnp.maximum(_dot(m, wu_ref[:, c * fc:(c + 1) * fc]), 0.0)
        acc = acc + _dot((a * a).astype(BF16), wd_ref[c * fc:(c + 1) * fc, :])
    valid = _valid_rows(pl.program_id(0), tps, tm)
    o_ref[...] = jnp.where(valid, h + _rms(acc, npost_ref[...]), 0.0)


def _mlp_call(h, npre, wu, wd, npost, l, tm, tps):
    rows = h.shape[0]
    nbytes = 2 * (2 * tm * D_MODEL * 4) + (2 * D_MODEL * D_FF * 2) + 6 * tm * D_MODEL * 4
    return pl.pallas_call(
        functools.partial(_mlp_kernel, tm=tm, tps=tps, fc=1024),
        grid=(rows // tm,),
        in_specs=[pl.BlockSpec((tm, D_MODEL), lambda i: (i, 0)), _layer_spec((1, D_MODEL), l),
                  _layer_spec((D_MODEL, D_FF), l), _layer_spec((D_FF, D_MODEL), l),
                  _layer_spec((1, D_MODEL), l)],
        out_specs=pl.BlockSpec((tm, D_MODEL), lambda i: (i, 0)),
        out_shape=jax.ShapeDtypeStruct((rows, D_MODEL), F32),
        compiler_params=pltpu.CompilerParams(
            dimension_semantics=("parallel",), vmem_limit_bytes=_vmem_limit(nbytes)),
        name="mlp",
    )(h, npre, wu, wd, npost)


def _prep_w_in(w_in):
    depth = w_in.shape[0]
    perm = np.zeros(256, np.int64)
    for half in range(2):
        for h in range(RET_HEADS):
            for i in range(32):
                perm[half * 128 + h * 32 + i] = h * 64 + half * 32 + i
    rq = w_in[:, :, 0:256][:, :, perm]
    rk = w_in[:, :, 256:512][:, :, perm]
    rest = w_in[:, :, 512:3584]
    hg = w_in[:, :, 3592:5640]
    dt = jnp.pad(w_in[:, :, 3584:3592], ((0, 0), (0, 0), (0, GW - SSD_HEADS)))
    wa = jnp.concatenate([rq, rk, rest, hg, dt], axis=2).astype(BF16)
    wg = w_in[:, :, 5640:].astype(BF16)
    assert wa.shape == (depth, D_MODEL, NA)
    return wa, wg


def _prep_s5(lam_re, lam_im, b_re, b_im, c_re, c_im, log_step):
    depth = lam_re.shape[0]
    step = jnp.exp(log_step.astype(F32))[..., None]
    lr, li = lam_re.astype(F32), lam_im.astype(F32)
    mag = jnp.exp(lr * step)
    ab_re, ab_im = mag * jnp.cos(li * step), mag * jnp.sin(li * step)
    inv = 1.0 / (lr * lr + li * li)
    co_re = ((ab_re - 1.0) * lr + ab_im * li) * inv
    co_im = (ab_im * lr - (ab_re - 1.0) * li) * inv
    br, bi = b_re.astype(F32), b_im.astype(F32)
    bb_re = co_re[..., None] * br - co_im[..., None] * bi
    bb_im = co_re[..., None] * bi + co_im[..., None] * br
    gps = S5_GROUPS // S5_SG
    n_tiles = 2 * S5_P2 // MXU_T
    gpt = MXU_T // S5_STATE
    bb = jnp.stack([bb_re, bb_im], axis=1).reshape(depth, 2, S5_SG, 2, gpt, S5_STATE, S5_GROUP)
    bb = bb.transpose(0, 2, 1, 3, 4, 6, 5)
    slab = bb[..., None, :] * jnp.eye(gpt, dtype=F32)[:, None, :, None]
    slab = slab.reshape(depth, n_tiles, gpt * S5_GROUP, MXU_T)
    nt = np.arange(n_tiles)
    at = np.zeros((n_tiles, MXU_T // (gpt * S5_GROUP)), np.float32)
    at[nt, 2 * ((nt // 4) % 2) + nt % 2] = 1.0
    bm = slab[:, :, None, :, :] * jnp.asarray(at)[None, :, :, None, None]
    bm = bm.reshape(depth, n_tiles, MXU_T, MXU_T).astype(BF16)
    cc = jnp.stack([c_re.astype(F32), c_im.astype(F32) * (-1.0)], axis=1)
    cc = cc.reshape(depth, 2, S5_SG, gps, S5_GROUP, S5_STATE).transpose(0, 2, 1, 3, 5, 4)
    cblk = cc[..., None, :] * jnp.eye(gps, dtype=F32)[:, None, :, None]
    cblk = cblk.reshape(depth, S5_SG, S5_SGW, gps * S5_GROUP)
    ac = np.zeros((S5_SG, MXU_T // (gps * S5_GROUP)), np.float32)
    ac[np.arange(S5_SG), np.arange(S5_SG) % 2] = 1.0
    cm = cblk[:, :, :, None, :] * jnp.asarray(ac)[None, :, None, :, None]
    cm = cm.reshape(depth, S5_SG, S5_SGW, MXU_T).astype(BF16)
    ar = jnp.broadcast_to(ab_re.reshape(depth, 1, S5_P2), (depth, NB, S5_P2))
    ai = jnp.broadcast_to(ab_im.reshape(depth, 1, S5_P2), (depth, NB, S5_P2))
    return bm, ar, ai, cm


def _rope_tables(tp):
    half = RET_DK // 2
    inv_freq = RET_ROPE_BASE ** (-jnp.arange(half, dtype=F32) / half)
    pos = jnp.maximum(jnp.arange(tp, dtype=F32) - PADF, 0.0)
    ang = pos[:, None] * inv_freq[None, :]
    return jnp.tile(jnp.cos(ang), (1, RET_HEADS)), jnp.tile(jnp.sin(ang), (1, RET_HEADS))


def _pick_tiles(tp):
    tt = CH
    tm = next(c for c in (544, 256, CH) if tp % c == 0)
    return tt, tm


def kernel(x, meta_tokens, w_in, w_branch, w_out, norm_pre_mix, norm_post_mix, norm_pre_mlp, norm_post_mlp, w_up, w_down, ret_gn_w, s5_lam_re, s5_lam_im, s5_b_re, s5_b_im, s5_c_re, s5_c_im, s5_d, s5_log_step, s5_glu_w, s5_glu_b, ssd_conv_w, ssd_conv_b, ssd_dt_bias, ssd_a_log, ssd_d, ssd_norm_w, hgrn_lb, hgrn_norm_w):
    bsz, seq, _ = x.shape
    depth = w_in.shape[0]
    assert bsz == NB and x.shape[2] == D_MODEL
    tp = PADF + N_META + seq
    assert tp % CH == 0
    tt, tm = _pick_tiles(tp)
    rows = tp * NB

    h = _assemble(x, meta_tokens)

    wa, wg = _prep_w_in(w_in)
    wb = w_branch.astype(BF16)
    wo = w_out.astype(BF16)
    wu = w_up.astype(BF16)
    wd = w_down.astype(BF16)
    r3 = lambda a: a.astype(F32).reshape(depth, 1, -1)
    npre, npost, mpre, mpost = r3(norm_pre_mix), r3(norm_post_mix), r3(norm_pre_mlp), r3(norm_post_mlp)
    bm, ar, ai, cm = _prep_s5(s5_lam_re, s5_lam_im, s5_b_re, s5_b_im, s5_c_re, s5_c_im, s5_log_step)
    s5gw = s5_glu_w.astype(BF16)
    s5gb = r3(s5_glu_b)
    s5d = r3(s5_d)
    zc = jnp.zeros((depth, SSD_CONV, GW), F32)
    cw = jnp.concatenate([zc] * G_XS + [ssd_conv_w[:, :, :GW].astype(F32), ssd_conv_w[:, :, GW:].astype(F32)]
                         + [zc] * (N_GROUPS - G_BC - 1), axis=2)
    zb = jnp.zeros((depth, 1, GW), F32)
    cb = jnp.concatenate([zb] * G_XS + [r3(ssd_conv_b[:, :GW]), r3(ssd_conv_b[:, GW:])]
                         + [zb] * (N_GROUPS - G_BC - 1), axis=2)
    pad8 = lambda a: jnp.pad(a.astype(F32), ((0, 0), (0, 128 - SSD_HEADS))).reshape(depth, 1, 128)
    dtb = pad8(ssd_dt_bias)
    ssd_a = pad8(-jnp.exp(ssd_a_log.astype(F32)))
    ssd_dsk = r3(jnp.repeat(ssd_d.astype(F32), SSD_HEADDIM, axis=1))
    ssd_nw = r3(ssd_norm_w)
    lb_all = jnp.cumsum(jax.nn.softmax(hgrn_lb.astype(F32), axis=0), axis=0)
    lb_all = r3(lb_all - lb_all[0])
    hg_nw = r3(hgrn_norm_w)
    gnw = r3(ret_gn_w)

    cos_t, sin_t = _rope_tables(tp)
    ret_tabs, ssd_tabs, hg_tabs = _ret_tables(), _ssd_tables(), _hg_tables()

    flat = lambda y: y.reshape(rows, GW)
    v3 = lambda y: y.reshape(NB, tp, y.shape[-1])
    for l in range(depth):
        proj, lfl, hk = _inproj(h, npre, wa, cos_t, sin_t, cw, cb, lb_all, dtb, ssd_a, l, tp)
        proj3 = v3(proj)
        y_ret = _ret_call(proj3, ret_tabs, gnw, l, tt)
        y_s5 = _s5_call(proj3, bm, ar, ai, cm, s5d, s5gw, s5gb, l, S5_TT)
        y_ssd = _ssd_call(proj3, ssd_tabs, ssd_dsk, ssd_nw, l, tt)
        y_hg = _hg_call(proj3, v3(lfl), v3(hk), hg_tabs, hg_nw, l, tt)
        h = _merge_call(h, (flat(y_ret), flat(y_s5), flat(y_ssd), flat(y_hg)),
                        npre, wg, wb, wo, npost, l, tm, tp // tm)
        h = _mlp_call(h, mpre, wu, wd, mpost, l, tm, tp // tm)

    return h.reshape(NB, tp, D_MODEL)[:, PADF + N_META:].astype(x.dtype)
```

```python
import functools

import numpy as np
import jax
import jax.numpy as jnp
from jax import lax
from jax.experimental import pallas as pl
from jax.experimental.pallas import tpu as pltpu

F32 = jnp.float32
BF16 = jnp.bfloat16

D_MODEL = 1024
N_META = 16
BRANCH_WIDTH = 512
N_BRANCH = 4
D_FF = 4 * D_MODEL
NORM_EPS = 1e-6
RET_HEADS = 4
RET_DK = 64
RET_DV = 128
RET_ROPE_BASE = 10000.0
S5_GROUP = 16
S5_GROUPS = 32
S5_STATE = 64
SSD_HEADDIM = 64
SSD_HEADS = 8
SSD_GROUPS = 2
SSD_STATE = 128
SSD_CONV = 4
HG_HEADS = 4
HG_DK = 128

NB = 8
PADF = 112
CH = 128
S5_TT = 64
GW = 512
N_GROUPS = 12
NA = GW * N_GROUPS
G_QK, G_RV, G_RG, G_S5, G_Z, G_XS, G_BC, G_HQ, G_HF, G_HI, G_HG, G_DT = range(N_GROUPS)
S5_P2 = S5_GROUPS * S5_STATE
S5_SG = 4
S5_SGW = 2 * S5_P2 // S5_SG
MXU_T = 256
NBP = 8
VMEM_LIMIT_CAP = 56 * 1024 * 1024


def _vmem_limit(nbytes):
    return int(min(VMEM_LIMIT_CAP, max(32 * 1024 * 1024, nbytes * 5 // 4 + 8 * 1024 * 1024)))


def _const_spec(shape):
    nd = len(shape)
    return pl.BlockSpec(shape, lambda *a: (0,) * nd, pipeline_mode=pl.Buffered(1))


def _layer_spec(shape, l):
    nd = len(shape)
    return pl.BlockSpec((None,) + tuple(shape), lambda *a: (l,) + (0,) * nd,
                        pipeline_mode=pl.Buffered(1))


def _split_bf16(x):
    hi = x.astype(BF16)
    lo = (x - hi.astype(F32)).astype(BF16)
    return hi, lo


def _dot(a, b):
    return jnp.dot(a, b, preferred_element_type=F32)


def _dot_nt(a, b):
    return lax.dot_general(a, b, (((1,), (1,)), ((), ())), preferred_element_type=F32)


def _dot_tn(a, b):
    return lax.dot_general(a, b, (((0,), (0,)), ((), ())), preferred_element_type=F32)


def _dot_sel(x, e_bf16):
    hi, lo = _split_bf16(x)
    return _dot(hi, e_bf16) + _dot(lo, e_bf16)


def _sigmoid(x):
    return 1.0 / (1.0 + jnp.exp(-x))


def _silu(x):
    return x * _sigmoid(x)


def _rms(x, w):
    ms = jnp.mean(x * x, axis=-1, keepdims=True)
    return x * lax.rsqrt(ms + NORM_EPS) * w


def _valid_rows(tile_idx, tiles_per_seq, tm):
    tpos = (tile_idx % tiles_per_seq) * tm + lax.broadcasted_iota(jnp.int32, (tm, 1), 0)
    return tpos >= PADF


def _assemble_kernel(x_ref, meta_ref, o_ref):
    o_ref[0:PADF, :] = jnp.zeros((PADF, D_MODEL), F32)
    o_ref[PADF:PADF + N_META, :] = meta_ref[...].astype(F32)
    o_ref[PADF + N_META:, :] = x_ref[...].astype(F32)


def _assemble(x, meta_tokens):
    bsz, seq, _ = x.shape
    tp = PADF + N_META + seq
    return pl.pallas_call(
        _assemble_kernel,
        grid=(bsz,),
        in_specs=[pl.BlockSpec((seq, D_MODEL), lambda b: (b, 0)), _const_spec((N_META, D_MODEL))],
        out_specs=pl.BlockSpec((tp, D_MODEL), lambda b: (b, 0)),
        out_shape=jax.ShapeDtypeStruct((bsz * tp, D_MODEL), F32),
        compiler_params=pltpu.CompilerParams(
            dimension_semantics=("parallel",),
            vmem_limit_bytes=_vmem_limit(2 * (seq + tp) * D_MODEL * 4)),
        name="assemble",
    )(x.reshape(bsz * seq, D_MODEL), meta_tokens)


def _inproj_kernel(h_ref, nw_ref, w_ref, cos_ref, sin_ref, cw_ref, cb_ref, lb_ref, dtb_ref, a_ref,
                   o_ref, lfl_ref, hk_ref, xn_ref, *, tm):
    j = pl.program_id(1)

    @pl.when(j == 0)
    def _():
        xn_ref[...] = _rms(h_ref[...], nw_ref[...]).astype(BF16)

    real = pl.ds(PADF, tm - PADF)

    def mm():
        return _dot(xn_ref[real, :], w_ref[...])

    def zero_pad(ref):
        ref[0:PADF, :] = jnp.zeros((PADF, GW), BF16)

    @pl.when((j == G_RV) | (j == G_S5) | (j == G_HI))
    def _():
        o_ref[real, :] = mm().astype(BF16)
        zero_pad(o_ref)

    @pl.when((j == G_RG) | (j == G_Z) | (j == G_HQ) | (j == G_HG))
    def _():
        o_ref[real, :] = _silu(mm()).astype(BF16)
        zero_pad(o_ref)

    @pl.when((j == G_XS) | (j == G_BC))
    def _():
        r = _dot(xn_ref[...], w_ref[...])
        acc = cb_ref[...] + r * cw_ref[SSD_CONV - 1:SSD_CONV, :]
        for back in range(1, SSD_CONV):
            tap = SSD_CONV - 1 - back
            acc = acc + pltpu.roll(r, back, 0) * cw_ref[tap:tap + 1, :]
        valid = _valid_rows(0, 1, tm)
        o_ref[...] = jnp.where(valid, _silu(acc), 0.0).astype(BF16)

    @pl.when(j == G_QK)
    def _():
        r = mm()
        cs = cos_ref[real, :]
        sn = sin_ref[real, :]
        q1, q2, k1, k2 = r[:, 0:128], r[:, 128:256], r[:, 256:384], r[:, 384:512]
        o_ref[real, 0:128] = (q1 * cs - q2 * sn).astype(BF16)
        o_ref[real, 128:256] = (q1 * sn + q2 * cs).astype(BF16)
        o_ref[real, 256:384] = ((k1 * cs - k2 * sn) * (RET_DK ** -0.5)).astype(BF16)
        o_ref[real, 384:512] = ((k1 * sn + k2 * cs) * (RET_DK ** -0.5)).astype(BF16)
        zero_pad(o_ref)

    @pl.when(j == G_HF)
    def _():
        lb = lb_ref[...]
        f = lb + (1.0 - lb) * _sigmoid(mm())
        hi, lo = _split_bf16(jnp.log(f))
        o_ref[real, :] = hi
        lfl_ref[real, :] = lo
        hk_ref[real, :] = (1.0 - f).astype(BF16)
        zero_pad(o_ref)
        zero_pad(lfl_ref)
        zero_pad(hk_ref)

    @pl.when(j == G_DT)
    def _():
        dt = jax.nn.softplus(mm()[:, 0:128] + dtb_ref[...])
        dhi, dlo = _split_bf16(dt)
        lhi, llo = _split_bf16(dt * a_ref[...])
        o_ref[real, 0:128] = dhi
        o_ref[real, 128:256] = dlo
        o_ref[real, 256:384] = lhi
        o_ref[real, 384:512] = llo
        zero_pad(o_ref)


def _inproj(h, nw, w, cos_t, sin_t, cw, cb, lb, dtb, a, l, tm):
    rows = h.shape[0]
    assert cos_t.shape[0] == tm
    nbytes = (2 * tm * D_MODEL * 4 + tm * D_MODEL * 2 + 2 * D_MODEL * GW * 2 + 3 * 2 * tm * GW * 2
              + 2 * tm * 128 * 4 + 4 * tm * GW * 4)
    ospec0 = pl.BlockSpec((tm, GW), lambda i, j: (i, 0))
    return pl.pallas_call(
        functools.partial(_inproj_kernel, tm=tm),
        grid=(rows // tm, N_GROUPS),
        in_specs=[
            pl.BlockSpec((tm, D_MODEL), lambda i, j: (i, 0)),
            pl.BlockSpec((None, 1, D_MODEL), lambda i, j: (l, 0, 0)),
            pl.BlockSpec((None, D_MODEL, GW), lambda i, j: (l, 0, j)),
            _const_spec((tm, 128)), _const_spec((tm, 128)),
            pl.BlockSpec((None, SSD_CONV, GW), lambda i, j: (l, 0, j)),
            pl.BlockSpec((None, 1, GW), lambda i, j: (l, 0, j)),
            _layer_spec((1, GW), l), _layer_spec((1, 128), l), _layer_spec((1, 128), l),
        ],
        out_specs=[pl.BlockSpec((tm, GW), lambda i, j: (i, j)), ospec0, ospec0],
        out_shape=[jax.ShapeDtypeStruct((rows, NA), BF16), jax.ShapeDtypeStruct((rows, GW), BF16),
                   jax.ShapeDtypeStruct((rows, GW), BF16)],
        scratch_shapes=[pltpu.VMEM((tm, D_MODEL), BF16)],
        compiler_params=pltpu.CompilerParams(
            dimension_semantics=("parallel", "arbitrary"), vmem_limit_bytes=_vmem_limit(nbytes)),
        name="inproj",
    )(h, nw, w, cos_t, sin_t, cw, cb, lb, dtb, a)


def _mixer_grp_spec(tt, g):
    return pl.BlockSpec((NBP, tt, GW), lambda bb, i, g=g: (bb, i, g))


def _mixer_params(est_bytes):
    return pltpu.CompilerParams(dimension_semantics=("parallel", "arbitrary"),
                                vmem_limit_bytes=_vmem_limit(est_bytes))


def _ret_tables():
    gam = 1.0 - np.exp2(-5.0 - np.arange(RET_HEADS, dtype=np.float64))
    lg = np.log(gam)
    t = np.arange(CH)
    col_head = (np.arange(2 * 128) % 128) // 32
    dall = np.zeros((CH, RET_HEADS * CH), np.float64)
    for h in range(RET_HEADS):
        d = np.exp(lg[h] * (t[:, None] - t[None, :]))
        dall[:, h * CH:(h + 1) * CH] = np.where(t[:, None] >= t[None, :], d, 0.0)
    qdec = np.exp(lg[col_head][None, :] * (t[:, None] + 1.0))
    kdec = np.exp(lg[col_head][None, :] * (CH - 1.0 - t[:, None]))
    vcol_head = np.arange(RET_HEADS * RET_DV) // RET_DV
    sdec = np.exp(lg[col_head] * CH)[:, None] * np.ones((1, 512))
    smask = (col_head[:, None] == vcol_head[None, :]).astype(np.float64)
    kmask = (col_head[None, :] == np.arange(RET_HEADS)[:, None]).astype(np.float64)[:, None, :]
    vmask = (vcol_head[None, :] == np.arange(RET_HEADS)[:, None]).astype(np.float64)[:, None, :]
    f = lambda a: jnp.asarray(a, F32)
    b = lambda a: jnp.asarray(a, BF16)
    return f(dall), f(qdec), f(kdec), f(sdec), f(smask), b(kmask), b(vmask)


def _ret_kernel(qk_ref, v_ref, g_ref, dall_ref, qdec_ref, kdec_ref, sdec_ref,
                smask_ref, kmask_ref, vmask_ref, gnw_ref, o_ref, st_ref, *, nch):
    @pl.when(pl.program_id(1) == 0)
    def _():
        st_ref[...] = jnp.zeros_like(st_ref)

    def one(s, rows):
        qk = qk_ref[s, rows, :]
        qb = qk[:, 0:256]
        kb = qk[:, 256:512]
        vb = v_ref[s, rows, :]
        kstack = jnp.concatenate([kb * kmask_ref[h] for h in range(RET_HEADS)], axis=0)
        s_all = _dot_nt(qb, kstack)
        p = (s_all * dall_ref[...]).astype(BF16)
        vstack = jnp.concatenate([vb * vmask_ref[h] for h in range(RET_HEADS)], axis=0)
        y = _dot(p, vstack)
        st = st_ref[s]
        y = y + _dot((qb.astype(F32) * qdec_ref[...]).astype(BF16), st.astype(BF16))
        new = _dot_tn((kb.astype(F32) * kdec_ref[...]).astype(BF16), vb)
        st_ref[s] = st * sdec_ref[...] + new * smask_ref[...]
        outs = []
        for h in range(RET_HEADS):
            yh = y[:, h * RET_DV:(h + 1) * RET_DV]
            mu = jnp.mean(yh, axis=-1, keepdims=True)
            d = yh - mu
            var = jnp.mean(d * d, axis=-1, keepdims=True)
            outs.append(d * lax.rsqrt(var + NORM_EPS))
        yn = jnp.concatenate(outs, axis=1) * gnw_ref[...]
        o_ref[s, rows, :] = g_ref[s, rows, :].astype(F32) * yn

    def chunk(c, carry):
        rows = pl.ds(pl.multiple_of(c * CH, CH), CH)
        for s in range(NBP):
            one(s, rows)
        return carry

    lax.fori_loop(0, nch, chunk, 0)


def _ret_call(proj3, tabs, gnw, l, tt):
    tp = proj3.shape[1]
    dall, qdec, kdec, sdec, smask, kmask, vmask = tabs
    return pl.pallas_call(
        functools.partial(_ret_kernel, nch=tt // CH),
        grid=(NB // NBP, tp // tt),
        in_specs=[_mixer_grp_spec(tt, G_QK), _mixer_grp_spec(tt, G_RV), _mixer_grp_spec(tt, G_RG),
                  _const_spec(dall.shape), _const_spec(qdec.shape), _const_spec(kdec.shape),
                  _const_spec(sdec.shape), _const_spec(smask.shape), _const_spec(kmask.shape),
                  _const_spec(vmask.shape), _layer_spec((1, GW), l)],
        out_specs=pl.BlockSpec((NBP, tt, GW), lambda bb, i: (bb, i, 0)),
        out_shape=jax.ShapeDtypeStruct((NB, tp, GW), F32),
        scratch_shapes=[pltpu.VMEM((NBP, 256, 512), F32)],
        compiler_params=_mixer_params(2 * NBP * tt * GW * (3 * 2 + 4) + (4 << 20)),
        name="retention",
    )(proj3, proj3, proj3, dall, qdec, kdec, sdec, smask, kmask, vmask, gnw)


def _s5_kernel(u_ref, bm_ref, ar_ref, ai_ref, cm_ref, d_ref, gw_ref, gb_ref, o_ref,
               utm_ref, ytm_ref, bu_ref, xs_ref, *, tt):
    @pl.when(pl.program_id(0) == 0)
    def _():
        xs_ref[...] = jnp.zeros_like(xs_ref)

    for b in range(NB):
        utm_ref[:, b, :] = u_ref[b].astype(F32)
    u = utm_ref[...].reshape(tt * NB, GW)
    ub = u.astype(BF16)
    half = S5_SGW // 2
    for nt in range(2 * S5_P2 // MXU_T):
        kb = nt // (2 * S5_P2 // MXU_T // (GW // MXU_T))
        bu_ref[:, nt * MXU_T:(nt + 1) * MXU_T] = _dot(ub[:, kb * MXU_T:(kb + 1) * MXU_T], bm_ref[nt])
    for c in range(S5_SG):
        cr = pl.ds(c * S5_SGW, half)
        ci = pl.ds(c * S5_SGW + half, half)
        ar = ar_ref[:, pl.ds(c * half, half)]
        ai = ai_ref[:, pl.ds(c * half, half)]

        def step(t, carry):
            xr, xi = carry
            rows = pl.ds(pl.multiple_of(t * NB, NB), NB)
            nr = ar * xr - ai * xi + bu_ref[rows, cr]
            ni = ar * xi + ai * xr + bu_ref[rows, ci]
            bu_ref[rows, cr] = nr
            bu_ref[rows, ci] = ni
            return nr, ni

        xr, xi = lax.fori_loop(0, tt, step, (xs_ref[:, cr], xs_ref[:, ci]), unroll=True)
        xs_ref[:, cr] = xr
        xs_ref[:, ci] = xi
    ys = []
    for j in range(GW // MXU_T):
        acc = None
        for c in range(j * (S5_SG // 2), (j + 1) * (S5_SG // 2)):
            part = _dot(bu_ref[:, c * S5_SGW:(c + 1) * S5_SGW].astype(BF16), cm_ref[c])
            acc = part if acc is None else acc + part
        ys.append(acc)
    y = jnp.concatenate(ys, axis=1) + d_ref[...] * u
    y = jax.nn.gelu(y)
    ag = _dot(y.astype(BF16), gw_ref[...]) + gb_ref[...]
    ytm_ref[...] = (ag[:, :BRANCH_WIDTH] * _sigmoid(ag[:, BRANCH_WIDTH:])).reshape(tt, NB, GW)
    for b in range(NB):
        o_ref[b] = ytm_ref[:, b, :]


def _s5_call(proj3, bm, ar, ai, cm, dsk, gw, gb, l, tt):
    tp = proj3.shape[1]
    rb = tt * NB
    nbytes = (2 * rb * GW * 2 + 2 * rb * GW * 4 + 2 * rb * GW * 4 + rb * 2 * S5_P2 * 4 * 2
              + (bm.shape[1] * MXU_T * MXU_T + cm.shape[1] * S5_SGW * MXU_T + GW * 2 * GW) * 2)
    return pl.pallas_call(
        functools.partial(_s5_kernel, tt=tt),
        grid=(tp // tt,),
        in_specs=[pl.BlockSpec((NB, tt, GW), lambda i: (0, i, G_S5)),
                  _layer_spec(bm.shape[1:], l), _layer_spec((NB, S5_P2), l),
                  _layer_spec((NB, S5_P2), l), _layer_spec(cm.shape[1:], l),
                  _layer_spec((1, GW), l), _layer_spec((GW, 2 * GW), l), _layer_spec((1, 2 * GW), l)],
        out_specs=pl.BlockSpec((NB, tt, GW), lambda i: (0, i, 0)),
        out_shape=jax.ShapeDtypeStruct((NB, tp, GW), F32),
        scratch_shapes=[pltpu.VMEM((tt, NB, GW), F32), pltpu.VMEM((tt, NB, GW), F32),
                        pltpu.VMEM((rb, 2 * S5_P2), F32), pltpu.VMEM((NB, 2 * S5_P2), F32)],
        compiler_params=pltpu.CompilerParams(
            dimension_semantics=("arbitrary",), vmem_limit_bytes=_vmem_limit(nbytes)),
        name="s5",
    )(proj3, bm, ar, ai, cm, dsk, gw, gb)


def _ssd_tables():
    t = np.arange(CH)
    ltri = (t[:, None] >= t[None, :]).astype(np.float32)
    expand = np.zeros((128, GW), np.float32)
    for h in range(SSD_HEADS):
        expand[h, h * SSD_HEADDIM:(h + 1) * SSD_HEADDIM] = 1.0
    hm = np.zeros((4, 1, 256), np.float32)
    for h in range(4):
        hm[h, 0, h * SSD_HEADDIM:(h + 1) * SSD_HEADDIM] = 1.0
    return jnp.asarray(ltri, BF16), jnp.asarray(expand, BF16), jnp.asarray(hm, BF16)


def _ssd_kernel(z_ref, xs_ref, bc_ref, dt_ref, ltri_ref, exp_ref, hm_ref, dsk_ref, nw_ref, o_ref,
                st_ref, *, nch):
    @pl.when(pl.program_id(1) == 0)
    def _():
        st_ref[...] = jnp.zeros_like(st_ref)

    tri = lax.broadcasted_iota(jnp.int32, (CH, CH), 0) >= lax.broadcasted_iota(jnp.int32, (CH, CH), 1)
    hpg = SSD_HEADS // SSD_GROUPS

    def one(s, rows):
        xs = xs_ref[s, rows, :].astype(F32)
        bc = bc_ref[s, rows, :]
        dtg = dt_ref[s, rows, :]
        cum = _dot(ltri_ref[...], dtg[:, 256:384]) + _dot(ltri_ref[...], dtg[:, 384:512])
        dt_e = _dot(dtg[:, 0:128], exp_ref[...]) + _dot(dtg[:, 128:256], exp_ref[...])
        cum_e = _dot_sel(cum, exp_ref[...])
        last_e = cum_e[CH - 1:CH, :]
        v = xs * dt_e
        vb = v.astype(BF16)
        wv = (v * jnp.exp(last_e - cum_e)).astype(BF16)
        ecum = jnp.exp(cum_e)
        elast = jnp.exp(last_e)
        cum_t = cum.T
        ys = []
        for g in range(SSD_GROUPS):
            gl = slice(g * 256, (g + 1) * 256)
            bm = bc[:, g * 128:(g + 1) * 128]
            cm = bc[:, 256 + g * 128:256 + (g + 1) * 128]
            scores = _dot_nt(cm, bm)
            ps = []
            for hh in range(hpg):
                h = g * hpg + hh
                seg = cum[:, h:h + 1] - cum_t[h:h + 1, :]
                ps.append(scores * jnp.exp(jnp.where(tri, seg, -jnp.inf)))
            pcat = jnp.concatenate(ps, axis=1).astype(BF16)
            vg = vb[:, gl]
            vstack = jnp.concatenate([vg * hm_ref[hh] for hh in range(hpg)], axis=0)
            st = st_ref[s, g]
            yg = _dot(pcat, vstack) + _dot(cm, st.astype(BF16)) * ecum[:, gl]
            st_ref[s, g] = st * elast[:, gl] + _dot_tn(bm, wv[:, gl])
            ys.append(yg)
        y = jnp.concatenate(ys, axis=1) + dsk_ref[...] * xs
        y = y * z_ref[s, rows, :].astype(F32)
        outs = [_rms(y[:, g * 256:(g + 1) * 256], nw_ref[:, g * 256:(g + 1) * 256])
                for g in range(SSD_GROUPS)]
        o_ref[s, rows, :] = jnp.concatenate(outs, axis=1)

    def chunk(c, carry):
        rows = pl.ds(pl.multiple_of(c * CH, CH), CH)
        for s in range(NBP):
            one(s, rows)
        return carry

    lax.fori_loop(0, nch, chunk, 0)


def _ssd_call(proj3, tabs, dsk, nw, l, tt):
    tp = proj3.shape[1]
    ltri, expand, hm = tabs
    return pl.pallas_call(
        functools.partial(_ssd_kernel, nch=tt // CH),
        grid=(NB // NBP, tp // tt),
        in_specs=[_mixer_grp_spec(tt, G_Z), _mixer_grp_spec(tt, G_XS), _mixer_grp_spec(tt, G_BC),
                  _mixer_grp_spec(tt, G_DT),
                  _const_spec(ltri.shape), _const_spec(expand.shape), _const_spec(hm.shape),
                  _layer_spec((1, GW), l), _layer_spec((1, GW), l)],
        out_specs=pl.BlockSpec((NBP, tt, GW), lambda bb, i: (bb, i, 0)),
        out_shape=jax.ShapeDtypeStruct((NB, tp, GW), F32),
        scratch_shapes=[pltpu.VMEM((NBP, SSD_GROUPS, SSD_STATE, 256), F32)],
        compiler_params=_mixer_params(2 * NBP * tt * GW * (4 * 2 + 4) + (4 << 20)),
        name="ssd",
    )(proj3, proj3, proj3, proj3, ltri, expand, hm, dsk, nw)


HG_LEVELS = tuple(CH >> i for i in range(CH.bit_length() - 1))
HG_MM_LEVELS = tuple(c for c in HG_LEVELS if c < 16)


def _hg_tables():
    t = np.arange(CH)
    blocks = []
    for c in HG_MM_LEVELS:
        m = np.zeros((CH, CH), np.float32)
        for r in range(CH):
            start = (r // c) * c
            mid = start + c // 2 - 1
            if r <= mid:
                m[r, r + 1:mid + 1] = 1.0
            else:
                m[r, mid + 1:r + 1] = 1.0
        blocks.append(m)
    blocks.append((t[:, None] >= t[None, :]).astype(np.float32))
    tall = np.concatenate(blocks, axis=0)
    ones_blk = np.kron(np.eye(HG_HEADS, dtype=np.float32), np.ones((HG_DK, HG_DK), np.float32))
    pmask = np.kron(np.eye(2, dtype=np.float32), np.ones((128, 128), np.float32))
    s2 = np.arange(2 * CH) % CH
    lm = np.stack([((t[:, None] // c == s2[None, :] // c) & (t[:, None] % c >= c // 2)
                    & (s2[None, :] % c < c // 2)).astype(np.float32) for c in HG_LEVELS])
    return (jnp.asarray(tall, BF16), jnp.asarray(ones_blk, BF16), jnp.asarray(pmask, F32),
            jnp.asarray(lm, F32))


def _hg_kernel(q_ref, lfh_ref, lfl_ref, k_ref, i_ref, g_ref, tall_ref, ones_ref, pmask_ref, lm_ref,
               nw_ref, o_ref, st_ref, *, nch):
    @pl.when(pl.program_id(1) == 0)
    def _():
        st_ref[...] = jnp.zeros_like(st_ref)

    row = lax.broadcasted_iota(jnp.int32, (CH, 1), 0)
    zero128 = jnp.zeros((CH, HG_DK), BF16)
    zero512 = jnp.zeros((CH, GW), BF16)

    def one(s, rows):
        qb = q_ref[s, rows, :]
        kb = k_ref[s, rows, :]
        vb = i_ref[s, rows, :]
        zall = _dot(tall_ref[...], lfh_ref[s, rows, :]) + _dot(tall_ref[...], lfl_ref[s, rows, :])
        nmm = len(HG_MM_LEVELS)
        cum = zall[nmm * CH:(nmm + 1) * CH, :]
        rem = cum[CH - 1:CH, :] - cum
        acc = [jnp.zeros((CH, 2 * CH), F32), jnp.zeros((CH, 2 * CH), F32)]
        for cs in HG_LEVELS:
            fh = (row % cs) < (cs // 2)
            if cs in HG_MM_LEVELS:
                li = HG_MM_LEVELS.index(cs)
                zl = zall[li * CH:(li + 1) * CH, :]
            else:
                c3 = cum.reshape(CH // cs, cs, GW)
                mid = c3[:, cs // 2 - 1:cs // 2, :]
                zl = jnp.concatenate([mid - c3[:, :cs // 2, :], c3[:, cs // 2:, :] - mid],
                                     axis=1).reshape(CH, GW)
            x = jnp.where(fh, kb, qb) * jnp.exp(zl).astype(BF16)
            qt = jnp.where(fh, zero512, x)
            kt = jnp.where(fh, x, zero512)
            lmask = lm_ref[HG_LEVELS.index(cs)]
            for p in range(2):
                a0 = kt[:, (2 * p) * 128:(2 * p + 1) * 128]
                a1 = kt[:, (2 * p + 1) * 128:(2 * p + 2) * 128]
                kstack = jnp.concatenate([jnp.concatenate([a0, zero128], axis=1),
                                          jnp.concatenate([zero128, a1], axis=1)], axis=0)
                sc = _dot_nt(qt[:, p * 256:(p + 1) * 256], kstack)
                acc[p] = acc[p] + sc * lmask
        r_same = _dot(qb * kb, ones_ref[...])
        qd = qb * jnp.exp(cum).astype(BF16)
        kw = kb * jnp.exp(rem).astype(BF16)
        elast = jnp.exp(cum[CH - 1:CH, :])
        ys = []
        for p in range(2):
            pl_ = slice(p * 256, (p + 1) * 256)
            v0 = vb[:, (2 * p) * 128:(2 * p + 1) * 128]
            v1 = vb[:, (2 * p + 1) * 128:(2 * p + 2) * 128]
            vstack = jnp.concatenate([jnp.concatenate([v0, zero128], axis=1),
                                      jnp.concatenate([zero128, v1], axis=1)], axis=0)
            st = st_ref[s, p]
            yp = _dot(acc[p].astype(BF16), vstack) + _dot_nt(qd[:, pl_], st.astype(BF16))
            st_ref[s, p] = st * elast[:, pl_] + _dot_tn(vb[:, pl_], kw[:, pl_]) * pmask_ref[...]
            ys.append(yp)
        o = jnp.concatenate(ys, axis=1) + r_same * vb.astype(F32)
        outs = [_rms(o[:, h * 128:(h + 1) * 128], nw_ref[:, h * 128:(h + 1) * 128])
                for h in range(HG_HEADS)]
        o_ref[s, rows, :] = jnp.concatenate(outs, axis=1) * g_ref[s, rows, :].astype(F32)

    def chunk(c, carry):
        rows = pl.ds(pl.multiple_of(c * CH, CH), CH)
        for s in range(NBP):
            one(s, rows)
        return carry

    lax.fori_loop(0, nch, chunk, 0)


def _hg_call(proj3, lfl3, hk3, tabs, nw, l, tt):
    tp = proj3.shape[1]
    tall, ones_blk, pmask, lm = tabs
    xspec = pl.BlockSpec((NBP, tt, GW), lambda bb, i: (bb, i, 0))
    return pl.pallas_call(
        functools.partial(_hg_kernel, nch=tt // CH),
        grid=(NB // NBP, tp // tt),
        in_specs=[_mixer_grp_spec(tt, G_HQ), _mixer_grp_spec(tt, G_HF), xspec, xspec,
                  _mixer_grp_spec(tt, G_HI), _mixer_grp_spec(tt, G_HG),
                  _const_spec(tall.shape), _const_spec(ones_blk.shape), _const_spec(pmask.shape),
                  _const_spec(lm.shape), _layer_spec((1, GW), l)],
        out_specs=pl.BlockSpec((NBP, tt, GW), lambda bb, i: (bb, i, 0)),
        out_shape=jax.ShapeDtypeStruct((NB, tp, GW), F32),
        scratch_shapes=[pltpu.VMEM((NBP, 2, 256, 256), F32)],
        compiler_params=_mixer_params(2 * NBP * tt * GW * (6 * 2 + 4) + (6 << 20)),
        name="hgrn2",
    )(proj3, proj3, lfl3, hk3, proj3, proj3, tall, ones_blk, pmask, lm, nw)


def _merge_kernel(h_ref, y0_ref, y1_ref, y2_ref, y3_ref, npre_ref, wg_ref, wb_ref, wo_ref,
                  npost_ref, o_ref, *, tm, tps):
    def body(r0):
        h = h_ref[r0:, :]
        u = _rms(h, npre_ref[...]).astype(BF16)
        mixed = jnp.zeros((tm - r0, D_MODEL), F32)
        for n, y_ref in enumerate((y0_ref, y1_ref, y2_ref, y3_ref)):
            gate = _sigmoid(_dot(u, wg_ref[:, n * D_MODEL:(n + 1) * D_MODEL]))
            mixed = mixed + gate * _dot(y_ref[r0:, :].astype(BF16), wb_ref[n])
        out = _dot(mixed.astype(BF16), wo_ref[...])
        o_ref[r0:, :] = h + _rms(out, npost_ref[...])
        if r0:
            o_ref[0:r0, :] = jnp.zeros((r0, D_MODEL), F32)

    first = (pl.program_id(0) % tps) == 0
    pl.when(first)(functools.partial(body, PADF))
    pl.when(jnp.logical_not(first))(functools.partial(body, 0))


def _merge_call(h, ys, npre, wg, wb, wo, npost, l, tm, tps):
    rows = h.shape[0]
    nbytes = (2 * (2 * tm * D_MODEL * 4 + 4 * tm * GW * 4)
              + (D_MODEL * 4 * D_MODEL + 4 * GW * D_MODEL + D_MODEL * D_MODEL) * 2
              + 6 * tm * D_MODEL * 4)
    yspec = pl.BlockSpec((tm, GW), lambda i: (i, 0))
    return pl.pallas_call(
        functools.partial(_merge_kernel, tm=tm, tps=tps),
        grid=(rows // tm,),
        in_specs=[pl.BlockSpec((tm, D_MODEL), lambda i: (i, 0)), yspec, yspec, yspec, yspec,
                  _layer_spec((1, D_MODEL), l), _layer_spec((D_MODEL, 4 * D_MODEL), l),
                  _layer_spec((N_BRANCH, GW, D_MODEL), l), _layer_spec((D_MODEL, D_MODEL), l),
                  _layer_spec((1, D_MODEL), l)],
        out_specs=pl.BlockSpec((tm, D_MODEL), lambda i: (i, 0)),
        out_shape=jax.ShapeDtypeStruct((rows, D_MODEL), F32),
        compiler_params=pltpu.CompilerParams(
            dimension_semantics=("parallel",), vmem_limit_bytes=_vmem_limit(nbytes)),
        name="merge",
    )(h, *ys, npre, wg, wb, wo, npost)


def _mlp_kernel(h_ref, npre_ref, wu_ref, wd_ref, npost_ref, o_ref, *, tm, tps, fc):
    def body(r0):
        h = h_ref[r0:, :]
        m = _rms(h, npre_ref[...]).astype(BF16)
        acc = jnp.zeros((tm - r0, D_MODEL), F32)
        for c in range(D_FF // fc):
            a = jnp.maximum(_dot(m, wu_ref[:, c * fc:(c + 1) * fc]), 0.0)
            acc = acc + _dot((a * a).astype(BF16), wd_ref[c * fc:(c + 1) * fc, :])
        o_ref[r0:, :] = h + _rms(acc, npost_ref[...])
        if r0:
            o_ref[0:r0, :] = jnp.zeros((r0, D_MODEL), F32)

    first = (pl.program_id(0) % tps) == 0
    pl.when(first)(functools.partial(body, PADF))
    pl.when(jnp.logical_not(first))(functools.partial(body, 0))


def _mlp_call(h, npre, wu, wd, npost, l, tm, tps):
    rows = h.shape[0]
    nbytes = 2 * (2 * tm * D_MODEL * 4) + (2 * D_MODEL * D_FF * 2) + 6 * tm * D_MODEL * 4
    return pl.pallas_call(
        functools.partial(_mlp_kernel, tm=tm, tps=tps, fc=1024),
        grid=(rows // tm,),
        in_specs=[pl.BlockSpec((tm, D_MODEL), lambda i: (i, 0)), _layer_spec((1, D_MODEL), l),
                  _layer_spec((D_MODEL, D_FF), l), _layer_spec((D_FF, D_MODEL), l),
                  _layer_spec((1, D_MODEL), l)],
        out_specs=pl.BlockSpec((tm, D_MODEL), lambda i: (i, 0)),
        out_shape=jax.ShapeDtypeStruct((rows, D_MODEL), F32),
        compiler_params=pltpu.CompilerParams(
            dimension_semantics=("parallel",), vmem_limit_bytes=_vmem_limit(nbytes)),
        name="mlp",
    )(h, npre, wu, wd, npost)


def _prep_w_in(w_in):
    depth = w_in.shape[0]
    perm = np.zeros(256, np.int64)
    for half in range(2):
        for h in range(RET_HEADS):
            for i in range(32):
                perm[half * 128 + h * 32 + i] = h * 64 + half * 32 + i
    rq = w_in[:, :, 0:256][:, :, perm]
    rk = w_in[:, :, 256:512][:, :, perm]
    rest = w_in[:, :, 512:3584]
    hg = w_in[:, :, 3592:5640]
    dt = jnp.pad(w_in[:, :, 3584:3592], ((0, 0), (0, 0), (0, GW - SSD_HEADS)))
    wa = jnp.concatenate([rq, rk, rest, hg, dt], axis=2).astype(BF16)
    wg = w_in[:, :, 5640:].astype(BF16)
    assert wa.shape == (depth, D_MODEL, NA)
    return wa, wg


def _prep_s5(lam_re, lam_im, b_re, b_im, c_re, c_im, log_step):
    depth = lam_re.shape[0]
    step = jnp.exp(log_step.astype(F32))[..., None]
    lr, li = lam_re.astype(F32), lam_im.astype(F32)
    mag = jnp.exp(lr * step)
    ab_re, ab_im = mag * jnp.cos(li * step), mag * jnp.sin(li * step)
    inv = 1.0 / (lr * lr + li * li)
    co_re = ((ab_re - 1.0) * lr + ab_im * li) * inv
    co_im = (ab_im * lr - (ab_re - 1.0) * li) * inv
    br, bi = b_re.astype(F32), b_im.astype(F32)
    bb_re = co_re[..., None] * br - co_im[..., None] * bi
    bb_im = co_re[..., None] * bi + co_im[..., None] * br
    gps = S5_GROUPS // S5_SG
    n_tiles = 2 * S5_P2 // MXU_T
    gpt = MXU_T // S5_STATE
    bb = jnp.stack([bb_re, bb_im], axis=1).reshape(depth, 2, S5_SG, 2, gpt, S5_STATE, S5_GROUP)
    bb = bb.transpose(0, 2, 1, 3, 4, 6, 5)
    slab = bb[..., None, :] * jnp.eye(gpt, dtype=F32)[:, None, :, None]
    slab = slab.reshape(depth, n_tiles, gpt * S5_GROUP, MXU_T)
    nt = np.arange(n_tiles)
    at = np.zeros((n_tiles, MXU_T // (gpt * S5_GROUP)), np.float32)
    at[nt, 2 * ((nt // 4) % 2) + nt % 2] = 1.0
    bm = slab[:, :, None, :, :] * jnp.asarray(at)[None, :, :, None, None]
    bm = bm.reshape(depth, n_tiles, MXU_T, MXU_T).astype(BF16)
    cc = jnp.stack([c_re.astype(F32), c_im.astype(F32) * (-1.0)], axis=1)
    cc = cc.reshape(depth, 2, S5_SG, gps, S5_GROUP, S5_STATE).transpose(0, 2, 1, 3, 5, 4)
    cblk = cc[..., None, :] * jnp.eye(gps, dtype=F32)[:, None, :, None]
    cblk = cblk.reshape(depth, S5_SG, S5_SGW, gps * S5_GROUP)
    ac = np.zeros((S5_SG, MXU_T // (gps * S5_GROUP)), np.float32)
    ac[np.arange(S5_SG), np.arange(S5_SG) % 2] = 1.0
    cm = cblk[:, :, :, None, :] * jnp.asarray(ac)[None, :, None, :, None]
    cm = cm.reshape(depth, S5_SG, S5_SGW, MXU_T).astype(BF16)
    ar = jnp.broadcast_to(ab_re.reshape(depth, 1, S5_P2), (depth, NB, S5_P2))
    ai = jnp.broadcast_to(ab_im.reshape(depth, 1, S5_P2), (depth, NB, S5_P2))
    return bm, ar, ai, cm


def _rope_tables(tp):
    half = RET_DK // 2
    inv_freq = RET_ROPE_BASE ** (-jnp.arange(half, dtype=F32) / half)
    pos = jnp.maximum(jnp.arange(tp, dtype=F32) - PADF, 0.0)
    ang = pos[:, None] * inv_freq[None, :]
    return jnp.tile(jnp.cos(ang), (1, RET_HEADS)), jnp.tile(jnp.sin(ang), (1, RET_HEADS))


def _pick_tiles(tp):
    tt = CH
    tm = next(c for c in (544, 256, CH) if tp % c == 0)
    return tt, tm


def kernel(x, meta_tokens, w_in, w_branch, w_out, norm_pre_mix, norm_post_mix, norm_pre_mlp, norm_post_mlp, w_up, w_down, ret_gn_w, s5_lam_re, s5_lam_im, s5_b_re, s5_b_im, s5_c_re, s5_c_im, s5_d, s5_log_step, s5_glu_w, s5_glu_b, ssd_conv_w, ssd_conv_b, ssd_dt_bias, ssd_a_log, ssd_d, ssd_norm_w, hgrn_lb, hgrn_norm_w):
    bsz, seq, _ = x.shape
    depth = w_in.shape[0]
    assert bsz == NB and x.shape[2] == D_MODEL
    tp = PADF + N_META + seq
    assert tp % CH == 0
    tt, tm = _pick_tiles(tp)
    rows = tp * NB

    h = _assemble(x, meta_tokens)

    wa, wg = _prep_w_in(w_in)
    wb = w_branch.astype(BF16)
    wo = w_out.astype(BF16)
    wu = w_up.astype(BF16)
    wd = w_down.astype(BF16)
    r3 = lambda a: a.astype(F32).reshape(depth, 1, -1)
    npre, npost, mpre, mpost = r3(norm_pre_mix), r3(norm_post_mix), r3(norm_pre_mlp), r3(norm_post_mlp)
    bm, ar, ai, cm = _prep_s5(s5_lam_re, s5_lam_im, s5_b_re, s5_b_im, s5_c_re, s5_c_im, s5_log_step)
    s5gw = s5_glu_w.astype(BF16)
    s5gb = r3(s5_glu_b)
    s5d = r3(s5_d)
    zc = jnp.zeros((depth, SSD_CONV, GW), F32)
    cw = jnp.concatenate([zc] * G_XS + [ssd_conv_w[:, :, :GW].astype(F32), ssd_conv_w[:, :, GW:].astype(F32)]
                         + [zc] * (N_GROUPS - G_BC - 1), axis=2)
    zb = jnp.zeros((depth, 1, GW), F32)
    cb = jnp.concatenate([zb] * G_XS + [r3(ssd_conv_b[:, :GW]), r3(ssd_conv_b[:, GW:])]
                         + [zb] * (N_GROUPS - G_BC - 1), axis=2)
    pad8 = lambda a: jnp.pad(a.astype(F32), ((0, 0), (0, 128 - SSD_HEADS))).reshape(depth, 1, 128)
    dtb = pad8(ssd_dt_bias)
    ssd_a = pad8(-jnp.exp(ssd_a_log.astype(F32)))
    ssd_dsk = r3(jnp.repeat(ssd_d.astype(F32), SSD_HEADDIM, axis=1))
    ssd_nw = r3(ssd_norm_w)
    lb_all = jnp.cumsum(jax.nn.softmax(hgrn_lb.astype(F32), axis=0), axis=0)
    lb_all = r3(lb_all - lb_all[0])
    hg_nw = r3(hgrn_norm_w)
    gnw = r3(ret_gn_w)

    cos_t, sin_t = _rope_tables(tp)
    ret_tabs, ssd_tabs, hg_tabs = _ret_tables(), _ssd_tables(), _hg_tables()

    flat = lambda y: y.reshape(rows, GW)
    v3 = lambda y: y.reshape(NB, tp, y.shape[-1])
    for l in range(depth):
        proj, lfl, hk = _inproj(h, npre, wa, cos_t, sin_t, cw, cb, lb_all, dtb, ssd_a, l, tp)
        proj3 = v3(proj)
        y_ret = _ret_call(proj3, ret_tabs, gnw, l, tt)
        y_s5 = _s5_call(proj3, bm, ar, ai, cm, s5d, s5gw, s5gb, l, S5_TT)
        y_ssd = _ssd_call(proj3, ssd_tabs, ssd_dsk, ssd_nw, l, tt)
        y_hg = _hg_call(proj3, v3(lfl), v3(hk), hg_tabs, hg_nw, l, tt)
        h = _merge_call(h, (flat(y_ret), flat(y_s5), flat(y_ssd), flat(y_hg)),
                        npre, wg, wb, wo, npost, l, tm, tp // tm)
        h = _mlp_call(h, mpre, wu, wd, mpost, l, tm, tp // tm)

    return h.reshape(NB, tp, D_MODEL)[:, PADF + N_META:].astype(x.dtype)
```
